```python
import math
import jax, jax.numpy as jnp
from jax import lax
import numpy as np

D_MODEL = 2048
BATCH = 1
SEQ = 8192
DEPTH = 1
DEC_BATCH = 32
DEC_SEQ = 1
PAST_LEN = 8192
PAGE_SIZE = 128

N_META = 16
DA_HEADS = 8
DA_QK = 64
DA_KD = 2 * DA_QK
DA_V = 2 * DA_QK
D_DA = DA_HEADS * DA_V
ML_HEADS = 8
ML_DK = 128
ML_DV = 128
ML_CHUNK = 64
D_ML = ML_HEADS * ML_DV
D_MIX = D_DA + D_ML
D_FF = 5632
CONV_W = 3
Q_BLOCK = 128
EPS = 1e-6
PROJ_SIZES = (DA_HEADS * DA_KD, DA_HEADS * DA_KD, D_DA,
              ML_HEADS * ML_DK, ML_HEADS * ML_DK, D_ML, D_ML, ML_HEADS, ML_HEADS)
D_IN = sum(PROJ_SIZES)

kernel_name = "hymba_diffattn_mlstm_convffn_step"


def rmsnorm(x, g):
    xf = x.astype(jnp.float32)
    y = xf * lax.rsqrt(jnp.mean(xf * xf, axis=-1, keepdims=True) + EPS)
    return (y * g.astype(jnp.float32)).astype(x.dtype)


def split_proj(h, w_in, b_i, b_f):
    B, T = h.shape[:2]
    z = h @ w_in
    dq, dk, dv, mq, mk, mv, mo, mi, mf = jnp.split(z, np.cumsum(PROJ_SIZES)[:-1], axis=-1)
    dq = dq.reshape(B, T, DA_HEADS, 2, DA_QK)
    dk = dk.reshape(B, T, DA_HEADS, DA_KD)
    dv = dv.reshape(B, T, DA_HEADS, DA_V)
    f32 = jnp.float32
    mq = mq.reshape(B, T, ML_HEADS, ML_DK).astype(f32)
    mk = mk.reshape(B, T, ML_HEADS, ML_DK).astype(f32) * (ML_DK ** -0.5)
    mv = mv.reshape(B, T, ML_HEADS, ML_DV).astype(f32)
    ig = mi.astype(f32) + b_i.astype(f32)
    lf = jax.nn.log_sigmoid(mf.astype(f32) + b_f.astype(f32))
    return dq, dk, dv, mq, mk, mv, mo, ig, lf


def da_logits(q, k, q_pos, k_pos):
    s = jnp.einsum('bqhcd,bkhcd->bhcqk', q, k).astype(jnp.float32) * (DA_QK ** -0.5)
    mask = k_pos[None, :] <= q_pos[:, None]
    return jnp.where(mask, s, -jnp.inf)


def da_mix(logit_list, v_list, lam):
    a = jax.nn.softmax(jnp.concatenate(logit_list, axis=-1), axis=-1)
    p = a[:, :, 0] - lam * a[:, :, 1]
    out = None
    off = 0
    for v in v_list:
        n = v.shape[1]
        part = jnp.einsum('bhqk,bkhd->bqhd', p[..., off:off + n].astype(v.dtype), v)
        out = part if out is None else out + part
        off += n
    return out


def da_prompt(q, k, v, lam):
    B, T = q.shape[:2]
    seq = T - N_META
    nb = seq // Q_BLOCK
    k_pos = jnp.arange(T)
    pos_m = jnp.arange(N_META)
    out_meta = da_mix([da_logits(q[:, :N_META], k[:, :N_META], pos_m, pos_m)], [v[:, :N_META]], lam)
    qr = q[:, N_META:].reshape(B, nb, Q_BLOCK, DA_HEADS, 2, DA_QK).swapaxes(0, 1)
    pr = (N_META + jnp.arange(seq)).reshape(nb, Q_BLOCK)

    def blk(args):
        qb, pb = args
        return da_mix([da_logits(qb, k, pb, k_pos)], [v], lam)

    out = lax.map(blk, (qr, pr))
    out = out.swapaxes(0, 1).reshape(B, seq, DA_HEADS, DA_V)
    return jnp.concatenate([out_meta, out], axis=1)


def da_sample(q, k_new, v_new, k_past, v_past, lam):
    S = q.shape[1]
    P = k_past.shape[1]
    q_pos = P + jnp.arange(S)
    lg_past = da_logits(q, k_past, q_pos, jnp.arange(P))
    lg_new = da_logits(q, k_new, q_pos, q_pos)
    return da_mix([lg_past, lg_new], [v_past, v_new], lam)


def mlstm_chunk(carry, inp):
    C, n, m = carry
    q, k, v, ig, lf = inp
    L = q.shape[1]
    b = jnp.cumsum(lf, axis=1)
    causal = jnp.tril(jnp.ones((L, L), dtype=bool))
    dmat = b[:, :, None, :] - b[:, None, :, :] + ig[:, None, :, :]
    dmat = jnp.where(causal[None, :, :, None], dmat, -jnp.inf)
    inter = b + m[:, None, :]
    m_t = jnp.maximum(inter, dmat.max(axis=2))
    w = jnp.exp(dmat - m_t[:, :, None, :])
    g = jnp.exp(inter - m_t)
    a = jnp.einsum('bthd,bshd->btsh', q, k) * w
    num = jnp.einsum('btsh,bshd->bthd', a, v) + g[..., None] * jnp.einsum('bthk,bhkv->bthv', q, C)
    den = a.sum(axis=2) + g * jnp.einsum('bthk,bhk->bth', q, n)
    h = num / jnp.maximum(jnp.abs(den), jnp.exp(-m_t))[..., None]
    m_new = m_t[:, -1]
    ws = jnp.exp(b[:, -1:] - b + ig - m_new[:, None])
    decay = jnp.exp(b[:, -1] + m - m_new)
    C_new = decay[..., None, None] * C + jnp.einsum('bsh,bshk,bshv->bhkv', ws, k, v)
    n_new = decay[..., None] * n + jnp.einsum('bsh,bshk->bhk', ws, k)
    return (C_new, n_new, m_new), h


def mlstm_prompt(q, k, v, ig, lf):
    B, T = q.shape[:2]
    seq = T - N_META
    nc = seq // ML_CHUNK
    f32 = jnp.float32
    carry = (jnp.zeros((B, ML_HEADS, ML_DK, ML_DV), f32), jnp.zeros((B, ML_HEADS, ML_DK), f32),
             jnp.zeros((B, ML_HEADS), f32))
    carry, h_meta = mlstm_chunk(carry, (q[:, :N_META], k[:, :N_META], v[:, :N_META], ig[:, :N_META], lf[:, :N_META]))

    def chunks(a):
        return a[:, N_META:].reshape((B, nc, ML_CHUNK) + a.shape[2:]).swapaxes(0, 1)

    carry, h_rest = lax.scan(mlstm_chunk, carry, (chunks(q), chunks(k), chunks(v), chunks(ig), chunks(lf)))
    h_rest = h_rest.swapaxes(0, 1).reshape(B, seq, ML_HEADS, ML_DV)
    return jnp.concatenate([h_meta, h_rest], axis=1), carry


def merge_groups(da_o, ml_h, ml_o, lam_init, g_da, g_ml, w_out, dtype):
    B, T = da_o.shape[:2]
    da = rmsnorm(da_o, g_da) * (1.0 - lam_init)
    ml = rmsnorm(ml_h, g_ml.reshape(ML_HEADS, ML_DV)) * jax.nn.sigmoid(
        ml_o.astype(jnp.float32)).reshape(B, T, ML_HEADS, ML_DV)
    y = jnp.concatenate([da.reshape(B, T, D_DA).astype(dtype), ml.reshape(B, T, D_ML).astype(dtype)], axis=-1)
    return y @ w_out


def conv_ffn(h, prefix, w_up, conv_w, conv_b, w_down):
    u = h @ w_up
    T = u.shape[1]
    u_pad = jnp.concatenate([prefix.astype(u.dtype), u], axis=1)
    y = conv_b + sum(conv_w[j] * u_pad[:, j:j + T] for j in range(CONV_W))
    a, g = jnp.split(y, 2, axis=-1)
    return (jax.nn.silu(a) * g) @ w_down, u_pad[:, T:]


def setup_inputs(seed: int = 0) -> dict:
    key = jax.random.key(seed)
    ks = jax.random.split(key, 32)
    f32 = jnp.float32
    n_pages = PAST_LEN // PAGE_SIZE
    n_used = DEC_BATCH * n_pages
    n_pool = n_used + max(1, n_used // 4)
    nrm = lambda k, s, sc=1.0: jax.random.normal(k, s, f32) * sc
    page_table = jax.random.permutation(ks[0], n_pool)[:n_used].reshape(DEC_BATCH, n_pages).astype(jnp.int32)
    b_f = jnp.linspace(3.0, 6.0, ML_HEADS, dtype=f32)[None] + nrm(ks[14], (DEPTH, ML_HEADS), 0.1)
    return {
        "x_prompt": nrm(ks[1], (BATCH, SEQ, D_MODEL)),
        "x_sample": nrm(ks[2], (DEC_BATCH, DEC_SEQ, D_MODEL)),
        "cache_k": nrm(ks[3], (DEPTH, n_pool, PAGE_SIZE, DA_HEADS, DA_KD)),
        "cache_v": nrm(ks[4], (DEPTH, n_pool, PAGE_SIZE, DA_HEADS, DA_V)),
        "page_table": page_table,
        "state_C": nrm(ks[5], (DEPTH, DEC_BATCH, ML_HEADS, ML_DK, ML_DV), 0.1),
        "state_n": nrm(ks[6], (DEPTH, DEC_BATCH, ML_HEADS, ML_DK), 0.1),
        "state_m": nrm(ks[7], (DEPTH, DEC_BATCH, ML_HEADS), 0.5),
        "state_conv": nrm(ks[8], (DEPTH, DEC_BATCH, CONV_W - 1, 2 * D_FF)),
        "meta_tokens": nrm(ks[9], (N_META, D_MODEL)),
        "g_mix": 1.0 + nrm(ks[10], (DEPTH, D_MODEL), 0.01),
        "w_in": nrm(ks[11], (DEPTH, D_MODEL, D_IN), D_MODEL ** -0.5),
        "b_i": nrm(ks[12], (DEPTH, ML_HEADS), 0.1),
        "b_f": b_f,
        "lam_q1": nrm(ks[15], (DEPTH, DA_QK), 0.1),
        "lam_k1": nrm(ks[16], (DEPTH, DA_QK), 0.1),
        "lam_q2": nrm(ks[17], (DEPTH, DA_QK), 0.1),
        "lam_k2": nrm(ks[18], (DEPTH, DA_QK), 0.1),
        "g_da": 1.0 + nrm(ks[19], (DEPTH, DA_V), 0.01),
        "g_ml": 1.0 + nrm(ks[20], (DEPTH, D_ML), 0.01),
        "w_out": nrm(ks[21], (DEPTH, D_MIX, D_MODEL), D_MIX ** -0.5),
        "g_ffn": 1.0 + nrm(ks[22], (DEPTH, D_MODEL), 0.01),
        "w_up": nrm(ks[23], (DEPTH, D_MODEL, 2 * D_FF), D_MODEL ** -0.5),
        "conv_w": nrm(ks[24], (DEPTH, CONV_W, 2 * D_FF), CONV_W ** -0.5),
        "conv_b": nrm(ks[25], (DEPTH, 2 * D_FF), 0.01),
        "w_down": nrm(ks[26], (DEPTH, D_FF, D_MODEL), D_FF ** -0.5),
        "g_final": 1.0 + nrm(ks[27], (D_MODEL,), 0.01),
    }


def reference(x_prompt, x_sample, cache_k, cache_v, page_table, state_C, state_n, state_m, state_conv,
              meta_tokens, g_mix, w_in, b_i, b_f, lam_q1, lam_k1, lam_q2, lam_k2, g_da, g_ml, w_out,
              g_ffn, w_up, conv_w, conv_b, w_down, g_final):
    dt = x_prompt.dtype
    B = x_prompt.shape[0]
    DB, S = x_sample.shape[:2]
    P = page_table.shape[1] * PAGE_SIZE
    xp = jnp.concatenate([jnp.broadcast_to(meta_tokens.astype(dt)[None], (B, N_META, D_MODEL)), x_prompt], axis=1)
    T = xp.shape[1]
    xs = x_sample
    kp_l, vp_l, ks_l, vs_l = [], [], [], []
    Cp_l, np_l, mp_l, Cs_l, ns_l, ms_l = [], [], [], [], [], []
    cp_l, cs_l = [], []
    for l in range(DEPTH):
        lam_init = 0.8 - 0.6 * math.exp(-0.3 * l)
        lam = (jnp.exp(jnp.sum(lam_q1[l].astype(jnp.float32) * lam_k1[l].astype(jnp.float32)))
               - jnp.exp(jnp.sum(lam_q2[l].astype(jnp.float32) * lam_k2[l].astype(jnp.float32))) + lam_init)
        h = rmsnorm(xp, g_mix[l])
        dq, dk, dv, mq, mk, mv, mo, ig, lf = split_proj(h, w_in[l], b_i[l], b_f[l])
        da_o = da_prompt(dq, dk.reshape(B, T, DA_HEADS, 2, DA_QK), dv, lam)
        ml_h, (Cp, n_p, mp) = mlstm_prompt(mq, mk, mv, ig, lf)
        xp = xp + merge_groups(da_o, ml_h, mo, lam_init, g_da[l], g_ml[l], w_out[l], dt)
        f, conv_p = conv_ffn(rmsnorm(xp, g_ffn[l]), jnp.zeros((B, CONV_W - 1, 2 * D_FF), dt),
                             w_up[l], conv_w[l], conv_b[l], w_down[l])
        xp = xp + f
        kp_l.append(dk); vp_l.append(dv)
        Cp_l.append(Cp); np_l.append(n_p); mp_l.append(mp); cp_l.append(conv_p)
        h = rmsnorm(xs, g_mix[l])
        dq, dk, dv, mq, mk, mv, mo, ig, lf = split_proj(h, w_in[l], b_i[l], b_f[l])
        k_past = cache_k[l][page_table].reshape(DB, P, DA_HEADS, 2, DA_QK)
        v_past = cache_v[l][page_table].reshape(DB, P, DA_HEADS, DA_V)
        da_o = da_sample(dq, dk.reshape(DB, S, DA_HEADS, 2, DA_QK), dv, k_past, v_past, lam)
        f32 = jnp.float32
        (Cs, n_s, ms), ml_h = mlstm_chunk((state_C[l].astype(f32), state_n[l].astype(f32), state_m[l].astype(f32)),
                                         (mq, mk, mv, ig, lf))
        xs = xs + merge_groups(da_o, ml_h, mo, lam_init, g_da[l], g_ml[l], w_out[l], dt)
        f, conv_s = conv_ffn(rmsnorm(xs, g_ffn[l]), state_conv[l], w_up[l], conv_w[l], conv_b[l], w_down[l])
        xs = xs + f
        ks_l.append(dk); vs_l.append(dv)
        Cs_l.append(Cs); ns_l.append(n_s); ms_l.append(ms); cs_l.append(conv_s)
    y_prompt = rmsnorm(xp, g_final)[:, N_META:]
    y_sample = rmsnorm(xs, g_final)
    return (y_prompt, y_sample,
            jnp.stack(kp_l), jnp.stack(vp_l), jnp.stack(ks_l), jnp.stack(vs_l),
            jnp.stack(Cp_l), jnp.stack(np_l), jnp.stack(mp_l),
            jnp.stack(Cs_l), jnp.stack(ns_l), jnp.stack(ms_l),
            jnp.stack(cp_l), jnp.stack(cs_l))
```

```python
import functools

import jax
import jax.numpy as jnp
from jax import lax
from jax.experimental import pallas as pl
from jax.experimental.pallas import tpu as pltpu

F32 = jnp.float32
BF16 = jnp.bfloat16

N_META = 16
N_HEADS = 8
HEAD_W = 128
DA_QK = 64
PAGE_SIZE = 128
CONV_W = 3
EPS = 1e-6
GROUP_W = N_HEADS * HEAD_W
LAM_INIT = 0.8 - 0.6 * 1.0
TAIL_ROWS = 128
NEG_INF = float("-inf")
FAR = 1 << 24

VMEM_LIMIT = 52 * 1024 * 1024

_NT = (((1,), (1,)), ((), ()))


def _cparams(n_axes):
    return pltpu.CompilerParams(dimension_semantics=("arbitrary",) * n_axes,
                                vmem_limit_bytes=VMEM_LIMIT)


def _pick(total, pref):
    t = min(pref, total)
    while total % t:
        t //= 2
    return t


def _pre_kernel(x_ref, g_ref, wg_ref, bias_ref, h_ref, ig_ref, lf_ref):
    x = x_ref[...]
    y = x * lax.rsqrt(jnp.mean(x * x, axis=-1, keepdims=True) + EPS) * g_ref[...]
    hb = y.astype(BF16)
    h_ref[...] = hb
    z = jnp.dot(hb, wg_ref[...], preferred_element_type=F32) + bias_ref[...]
    ig_ref[...] = z[:, :HEAD_W]
    zf = z[:, HEAD_W:]
    lf_ref[...] = jnp.minimum(zf, 0.0) - jnp.log1p(jnp.exp(-jnp.abs(zf)))


def _pre(x, g, wg, bias):
    m, d = x.shape
    tm = _pick(m, 512)
    return pl.pallas_call(
        _pre_kernel,
        grid=(m // tm,),
        in_specs=[pl.BlockSpec((tm, d), lambda i: (i, 0)),
                  pl.BlockSpec((1, d), lambda i: (0, 0)),
                  pl.BlockSpec((d, 2 * HEAD_W), lambda i: (0, 0)),
                  pl.BlockSpec((1, 2 * HEAD_W), lambda i: (0, 0))],
        out_specs=[pl.BlockSpec((tm, d), lambda i: (i, 0)),
                   pl.BlockSpec((tm, HEAD_W), lambda i: (i, 0)),
                   pl.BlockSpec((tm, HEAD_W), lambda i: (i, 0))],
        out_shape=[jax.ShapeDtypeStruct((m, d), BF16),
                   jax.ShapeDtypeStruct((m, HEAD_W), F32),
                   jax.ShapeDtypeStruct((m, HEAD_W), F32)],
        compiler_params=_cparams(1),
        name="pre_norm_gates",
    )(x, g, wg, bias)


def _proj_kernel(h_ref, w_ref, s_ref, *rest, with_f32):
    if with_f32:
        of_ref, ob_ref, wb_ref = rest
    else:
        ob_ref, wb_ref = rest

    @pl.when(pl.program_id(1) == 0)
    def _():
        wb_ref[...] = w_ref[...].astype(BF16)

    z = jnp.dot(h_ref[...], wb_ref[...], preferred_element_type=F32)
    if with_f32:
        of_ref[0] = z
    ob_ref[0] = (z * s_ref[0, :, 0:1]).astype(BF16)


def _proj(h, w_in, groups, scales, with_f32):
    m, d = h.shape
    tm = _pick(m, 512)
    ng = len(groups)
    first, rest_off = groups[0], groups[1] - 1 if ng > 1 else 0
    assert list(groups) == [first] + [rest_off + k for k in range(1, ng)]
    s_arr = jnp.broadcast_to(jnp.asarray(scales, F32)[:, None, None], (ng, 1, HEAD_W))

    def w_map(j, i):
        return (0, jnp.where(j == 0, first, j + rest_off))

    out_specs = [pl.BlockSpec((1, tm, GROUP_W), lambda j, i: (j, i, 0))]
    out_shape = [jax.ShapeDtypeStruct((ng, m, GROUP_W), BF16)]
    if with_f32:
        out_specs = [pl.BlockSpec((1, tm, GROUP_W), lambda j, i: (j, i, 0))] + out_specs
        out_shape = [jax.ShapeDtypeStruct((ng, m, GROUP_W), F32)] + out_shape
    outs = pl.pallas_call(
        functools.partial(_proj_kernel, with_f32=with_f32),
        grid=(ng, m // tm),
        in_specs=[pl.BlockSpec((tm, d), lambda j, i: (i, 0)),
                  pl.BlockSpec((d, GROUP_W), w_map),
                  pl.BlockSpec((1, 1, HEAD_W), lambda j, i: (j, 0, 0))],
        out_specs=out_specs,
        out_shape=out_shape,
        scratch_shapes=[pltpu.VMEM((d, GROUP_W), BF16)],
        compiler_params=_cparams(2),
        name="in_proj_f32" if with_f32 else "in_proj",
    )(h, w_in, s_arr)
    return outs if with_f32 else outs[0]


def _attn_kernel(qi_ref, kj_ref, q_ref, k_ref, v_ref, kp_ref, vp_ref, lam_ref, g_ref,
                 o_ref, m_sc, l_sc, acc_sc, *, tq, tk, q_pos0, prefix_off, n_steps):
    step = pl.program_id(1)
    i = qi_ref[step]
    j = kj_ref[step]
    q = q_ref[...]
    lane = lax.broadcasted_iota(jnp.int32, q.shape, 1)
    zero = jnp.zeros_like(q)
    q2 = jnp.concatenate([jnp.where(lane < DA_QK, q, zero), jnp.where(lane >= DA_QK, q, zero)], axis=0)

    def q_row(shape):
        r = lax.broadcasted_iota(jnp.int32, shape, 0)
        return jnp.where(r >= tq, r - tq, r)

    @pl.when(j == 0)
    def _():
        sp = lax.dot_general(q2, kp_ref[...], _NT, preferred_element_type=F32)
        col = lax.broadcasted_iota(jnp.int32, sp.shape, 1)
        ok = (col < N_META) & (col <= q_row(sp.shape) + prefix_off)
        sp = jnp.where(ok, sp, NEG_INF)
        m0 = jnp.max(sp, axis=1, keepdims=True)
        p = jnp.exp(sp - m0)
        m_sc[...] = m0
        l_sc[...] = jnp.sum(p, axis=1, keepdims=True)
        acc_sc[...] = jnp.dot(p.astype(BF16), vp_ref[...], preferred_element_type=F32)

    def block(masked):
        s = lax.dot_general(q2, k_ref[...], _NT, preferred_element_type=F32)
        if masked:
            col = lax.broadcasted_iota(jnp.int32, s.shape, 1)
            s = jnp.where(j * tk + col <= q_pos0 + i * tq + q_row(s.shape), s, NEG_INF)
        m_prev = m_sc[...]
        m_new = jnp.maximum(m_prev, jnp.max(s, axis=1, keepdims=True))
        alpha = jnp.exp(m_prev - m_new)
        p = jnp.exp(s - m_new)
        l_sc[...] = alpha * l_sc[...] + jnp.sum(p, axis=1, keepdims=True)
        acc_sc[...] = alpha * acc_sc[...] + jnp.dot(p.astype(BF16), v_ref[...], preferred_element_type=F32)
        m_sc[...] = m_new

    needs_mask = j * tk + (tk - 1) > q_pos0 + i * tq
    pl.when(needs_mask)(lambda: block(True))
    pl.when(jnp.logical_not(needs_mask))(lambda: block(False))

    is_last = jnp.logical_or(step == n_steps - 1, qi_ref[step + 1] != i)

    @pl.when(is_last)
    def _():
        o = acc_sc[...] / l_sc[...]
        lam = (jnp.exp(jnp.sum(lam_ref[0:1, :] * lam_ref[1:2, :], axis=1, keepdims=True))
               - jnp.exp(jnp.sum(lam_ref[2:3, :] * lam_ref[3:4, :], axis=1, keepdims=True)) + LAM_INIT)
        d = o[:tq] - lam * o[tq:]
        y = d * lax.rsqrt(jnp.mean(d * d, axis=-1, keepdims=True) + EPS) * g_ref[...]
        o_ref[...] = (y * (1.0 - LAM_INIT)).astype(o_ref.dtype)


def _attn(q, k, v, kp, vp, lam4, g_da, *, n_q_rows, tq, tk, q_pos0, prefix_off, pairs):
    qi = jnp.asarray([p[0] for p in pairs] + [-1], jnp.int32)
    kj = jnp.asarray([p[1] for p in pairs] + [0], jnp.int32)
    n_steps = len(pairs)
    kern = functools.partial(_attn_kernel, tq=tq, tk=tk, q_pos0=q_pos0, prefix_off=prefix_off,
                             n_steps=n_steps)
    grid_spec = pltpu.PrefetchScalarGridSpec(
        num_scalar_prefetch=2,
        grid=(N_HEADS, n_steps),
        in_specs=[pl.BlockSpec((tq, HEAD_W), lambda h, s, qi, kj: (qi[s], h)),
                  pl.BlockSpec((tk, HEAD_W), lambda h, s, qi, kj: (kj[s], h)),
                  pl.BlockSpec((tk, HEAD_W), lambda h, s, qi, kj: (kj[s], h)),
                  pl.BlockSpec((TAIL_ROWS, HEAD_W), lambda h, s, qi, kj: (0, h)),
                  pl.BlockSpec((TAIL_ROWS, HEAD_W), lambda h, s, qi, kj: (0, h)),
                  pl.BlockSpec((4, DA_QK), lambda h, s, qi, kj: (0, 0)),
                  pl.BlockSpec((1, HEAD_W), lambda h, s, qi, kj: (0, 0))],
        out_specs=pl.BlockSpec((tq, HEAD_W), lambda h, s, qi, kj: (qi[s], h)),
        scratch_shapes=[pltpu.VMEM((2 * tq, 1), F32), pltpu.VMEM((2 * tq, 1), F32),
                        pltpu.VMEM((2 * tq, HEAD_W), F32)])
    return pl.pallas_call(
        kern, grid_spec=grid_spec,
        out_shape=jax.ShapeDtypeStruct((n_q_rows, GROUP_W), BF16),
        compiler_params=_cparams(2),
        name="diff_attn_prompt",
    )(qi, kj, q, k, v, kp, vp, lam4, g_da)


def _mlstm_kernel(q_ref, k_ref, v_ref, o_ref, ig_ref, lf_ref, c0_ref, m0_ref, g_ref,
                  y_ref, c_out_ref, m_out_ref, c_sc, m_sc, *, chunk, n_chunks):
    c = pl.program_id(0)

    @pl.when(c == 0)
    def _():
        c_sc[...] = c0_ref[...]
        m_sc[...] = m0_ref[...]

    L = chunk
    ig = ig_ref[...]
    lf = lf_ref[...]
    row = lax.broadcasted_iota(jnp.int32, (L, L), 0)
    col = lax.broadcasted_iota(jnp.int32, (L, L), 1)
    causal = col <= row
    b_col = jnp.dot(causal.astype(F32), lf, preferred_element_type=F32,
                    precision=lax.Precision.HIGHEST)
    b_row = b_col.T
    ig_row = ig.T
    m_prev_all = m_sc[...]
    lane1 = lax.broadcasted_iota(jnp.int32, (1, HEAD_W), 1)
    m_next_all = m_prev_all
    ones = jnp.ones((L, HEAD_W), BF16)

    for h in range(N_HEADS):
        hs = slice(h * HEAD_W, (h + 1) * HEAD_W)
        bc = b_col[:, h:h + 1]
        m_prev = m_prev_all[:, h:h + 1]
        dm = jnp.where(causal, bc - b_row[h:h + 1, :] + ig_row[h:h + 1, :], NEG_INF)
        inter = bc + m_prev
        m_t = jnp.maximum(inter, jnp.max(dm, axis=1, keepdims=True))
        w = jnp.exp(dm - m_t)
        g = jnp.exp(inter - m_t)
        qh = q_ref[:, hs]
        kh = k_ref[:, hs]
        v_aug = jnp.concatenate([v_ref[:, hs], ones], axis=1)
        s = lax.dot_general(qh, kh, _NT, preferred_element_type=F32)
        a = (s * w).astype(BF16)
        c_h = c_sc[h]
        num_aug = (jnp.dot(a, v_aug, preferred_element_type=F32)
                   + g * jnp.dot(qh, c_h.astype(BF16), preferred_element_type=F32))
        num = num_aug[:, :HEAD_W]
        den = num_aug[:, HEAD_W:]
        hh = num / jnp.maximum(jnp.abs(den), jnp.exp(-m_t))
        y = hh * lax.rsqrt(jnp.mean(hh * hh, axis=-1, keepdims=True) + EPS) * g_ref[:, hs]
        y_ref[:, hs] = (y * jax.nn.sigmoid(o_ref[:, hs].astype(F32))).astype(y_ref.dtype)
        b_last = bc[L - 1:L]
        m_new = m_t[L - 1:L]
        ws = jnp.exp(b_last - bc + ig[:, h:h + 1] - m_new)
        decay = jnp.exp(b_last + m_prev - m_new)
        kw_t = (kh.astype(F32) * ws).T.astype(BF16)
        c_sc[h] = decay * c_h + jnp.dot(kw_t, v_aug, preferred_element_type=F32)
        m_next_all = jnp.where(lane1 == h, m_new, m_next_all)

    m_sc[...] = m_next_all

    @pl.when(c == n_chunks - 1)
    def _():
        c_out_ref[...] = c_sc[...]
        m_out_ref[...] = m_sc[...]


def _mlstm(q, k, v, o, ig, lf, c0, m0, g_ml, *, n_rows, chunk):
    n_chunks = n_rows // chunk
    row_blk = lambda w: pl.BlockSpec((chunk, w), lambda c: (c, 0))
    whole = lambda shape: pl.BlockSpec(shape, lambda c: (0,) * len(shape))
    return pl.pallas_call(
        functools.partial(_mlstm_kernel, chunk=chunk, n_chunks=n_chunks),
        grid=(n_chunks,),
        in_specs=[row_blk(GROUP_W), row_blk(GROUP_W), row_blk(GROUP_W), row_blk(GROUP_W),
                  row_blk(HEAD_W), row_blk(HEAD_W),
                  whole((N_HEADS, HEAD_W, 2 * HEAD_W)), whole((1, HEAD_W)), whole((1, GROUP_W))],
        out_specs=[row_blk(GROUP_W), whole((N_HEADS, HEAD_W, 2 * HEAD_W)), whole((1, HEAD_W))],
        out_shape=[jax.ShapeDtypeStruct((n_rows, GROUP_W), BF16),
                   jax.ShapeDtypeStruct((N_HEADS, HEAD_W, 2 * HEAD_W), F32),
                   jax.ShapeDtypeStruct((1, HEAD_W), F32)],
        scratch_shapes=[pltpu.VMEM((N_HEADS, HEAD_W, 2 * HEAD_W), F32), pltpu.VMEM((1, HEAD_W), F32)],
        compiler_params=_cparams(1),
        name="mlstm_prompt",
    )(q, k, v, o, ig, lf, c0, m0, g_ml)


def _decode_kernel(pt_ref, q_ref, kn_ref, vn_ref, lam_ref, g_ref, *rest, pages_per_step, n_steps):
    k_refs = rest[:pages_per_step]
    v_refs = rest[pages_per_step:2 * pages_per_step]
    o_ref, m_sc, l_sc, acc_sc = rest[2 * pages_per_step:]
    j = pl.program_id(1)
    n_rows = 2 * N_HEADS
    q = q_ref[0]
    rr = lax.broadcasted_iota(jnp.int32, (n_rows, GROUP_W), 0)
    cc = lax.broadcasted_iota(jnp.int32, (n_rows, GROUP_W), 1)
    qt = jnp.where(cc // DA_QK == rr, jnp.broadcast_to(q, (n_rows, GROUP_W)), 0.0)

    @pl.when(j == 0)
    def _():
        m_sc[...] = jnp.sum(qt * kn_ref[0], axis=1, keepdims=True)
        l_sc[...] = jnp.ones_like(l_sc)
        acc_sc[...] = jnp.broadcast_to(vn_ref[0], (n_rows, GROUP_W))

    qb = qt.astype(BF16)
    s = jnp.concatenate(
        [lax.dot_general(qb, kr[0].astype(BF16), _NT, preferred_element_type=F32) for kr in k_refs],
        axis=1)
    m_prev = m_sc[...]
    m_new = jnp.maximum(m_prev, jnp.max(s, axis=1, keepdims=True))
    alpha = jnp.exp(m_prev - m_new)
    p = jnp.exp(s - m_new)
    l_sc[...] = alpha * l_sc[...] + jnp.sum(p, axis=1, keepdims=True)
    pb = p.astype(BF16)
    pv = jnp.dot(pb[:, :PAGE_SIZE], v_refs[0][0].astype(BF16), preferred_element_type=F32)
    for g in range(1, pages_per_step):
        pv += jnp.dot(pb[:, g * PAGE_SIZE:(g + 1) * PAGE_SIZE], v_refs[g][0].astype(BF16),
                      preferred_element_type=F32)
    acc_sc[...] = alpha * acc_sc[...] + pv
    m_sc[...] = m_new

    @pl.when(j == n_steps - 1)
    def _():
        o = acc_sc[...] / l_sc[...]
        lam = (jnp.exp(jnp.sum(lam_ref[0:1, :] * lam_ref[1:2, :], axis=1, keepdims=True))
               - jnp.exp(jnp.sum(lam_ref[2:3, :] * lam_ref[3:4, :], axis=1, keepdims=True)) + LAM_INIT)
        outs = []
        for h in range(N_HEADS):
            hs = slice(h * HEAD_W, (h + 1) * HEAD_W)
            d = o[2 * h:2 * h + 1, hs] - lam * o[2 * h + 1:2 * h + 2, hs]
            y = d * lax.rsqrt(jnp.mean(d * d, axis=-1, keepdims=True) + EPS) * g_ref[...]
            outs.append(y * (1.0 - LAM_INIT))
        o_ref[0] = jnp.concatenate(outs, axis=1).astype(o_ref.dtype)


def _decode_attn(page_table, q, k_new, v_new, cache_k, cache_v, lam4, g_da):
    b, n_pages = page_table.shape
    pps = _pick(n_pages, 8)
    n_steps = n_pages // pps
    pt_flat = page_table.reshape(-1)

    def page_spec(g):
        return pl.BlockSpec((1, PAGE_SIZE, GROUP_W),
                            lambda bi, j, pt: (pt[bi * n_pages + j * pps + g], 0, 0))

    seq_spec = pl.BlockSpec((1, 1, GROUP_W), lambda bi, j, pt: (bi, 0, 0))
    grid_spec = pltpu.PrefetchScalarGridSpec(
        num_scalar_prefetch=1,
        grid=(b, n_steps),
        in_specs=[seq_spec, seq_spec, seq_spec,
                  pl.BlockSpec((4, DA_QK), lambda bi, j, pt: (0, 0)),
                  pl.BlockSpec((1, HEAD_W), lambda bi, j, pt: (0, 0))]
                 + [page_spec(g) for g in range(pps)] + [page_spec(g) for g in range(pps)],
        out_specs=seq_spec,
        scratch_shapes=[pltpu.VMEM((2 * N_HEADS, 1), F32), pltpu.VMEM((2 * N_HEADS, 1), F32),
                        pltpu.VMEM((2 * N_HEADS, GROUP_W), F32)])
    return pl.pallas_call(
        functools.partial(_decode_kernel, pages_per_step=pps, n_steps=n_steps),
        grid_spec=grid_spec,
        out_shape=jax.ShapeDtypeStruct((b, 1, GROUP_W), BF16),
        compiler_params=_cparams(2),
        name="diff_attn_decode",
    )(pt_flat, q, k_new, v_new, lam4, g_da, *([cache_k] * pps), *([cache_v] * pps))


def _mlstm_step_kernel(q_ref, k_ref, v_ref, o_ref, ig_ref, lf_ref, c_ref, n_ref, m_ref, g_ref,
                       y_ref, c_out_ref, n_out_ref, m_out_ref):
    q8 = q_ref[0]
    k8 = k_ref[0]
    v8 = v_ref[0]
    n8 = n_ref[0]
    ig = ig_ref[0]
    lf = lf_ref[0]
    m = m_ref[0]
    inter = lf + m
    m_t = jnp.maximum(inter, ig)
    w_all = jnp.exp(ig - m_t)
    g_all = jnp.exp(inter - m_t)
    floor_all = jnp.exp(-m_t)
    m_out_ref[0] = m_t
    q_t = q8.T
    k_t = k8.T
    qk = jnp.sum(q8 * k8, axis=1, keepdims=True)
    qn = jnp.sum(q8 * n8, axis=1, keepdims=True)
    for h in range(N_HEADS):
        w = w_all[:, h:h + 1]
        g = g_all[:, h:h + 1]
        c_h = c_ref[0, h]
        vh = v8[h:h + 1]
        a = qk[h:h + 1] * w
        q_c = jnp.sum(q_t[:, h:h + 1] * c_h, axis=0, keepdims=True)
        num = a * vh + g * q_c
        den = a + g * qn[h:h + 1]
        hh = num / jnp.maximum(jnp.abs(den), floor_all[:, h:h + 1])
        y = hh * lax.rsqrt(jnp.mean(hh * hh, axis=-1, keepdims=True) + EPS) * g_ref[h:h + 1]
        y_ref[0, h:h + 1] = (y * jax.nn.sigmoid(o_ref[0, h:h + 1].astype(F32))).astype(y_ref.dtype)
        c_out_ref[0, h] = g * c_h + w * (k_t[:, h:h + 1] * vh)
        n_out_ref[0, h:h + 1] = g * n8[h:h + 1] + w * k8[h:h + 1]


def _mlstm_step(q, k, v, o, ig, lf, c, n, m, g_ml):
    b = q.shape[0]
    hv = pl.BlockSpec((1, N_HEADS, HEAD_W), lambda i: (i, 0, 0))
    gate = pl.BlockSpec((1, 1, HEAD_W), lambda i: (i, 0, 0))
    cs = pl.BlockSpec((1, N_HEADS, HEAD_W, HEAD_W), lambda i: (i, 0, 0, 0))
    return pl.pallas_call(
        _mlstm_step_kernel,
        grid=(b,),
        in_specs=[hv, hv, hv, hv, gate, gate, cs, hv, gate,
                  pl.BlockSpec((N_HEADS, HEAD_W), lambda i: (0, 0))],
        out_specs=[hv, cs, hv, gate],
        out_shape=[jax.ShapeDtypeStruct((b, N_HEADS, HEAD_W), F32),
                   jax.ShapeDtypeStruct(c.shape, F32),
                   jax.ShapeDtypeStruct((b, N_HEADS, HEAD_W), F32),
                   jax.ShapeDtypeStruct((b, 1, HEAD_W), F32)],
        compiler_params=_cparams(1),
        name="mlstm_sample",
    )(q, k, v, o, ig, lf, c, n, m, g_ml)


def _out_proj_kernel(da_ref, ml_ref, x_ref, w_ref, o_ref, wb_ref):
    @pl.when(pl.program_id(1) == 0)
    def _():
        wb_ref[...] = w_ref[...].astype(BF16)

    y = jnp.concatenate([da_ref[...], ml_ref[...]], axis=1)
    o_ref[...] = x_ref[...] + jnp.dot(y, wb_ref[...], preferred_element_type=F32)


def _out_proj(da, ml, x, w_out):
    m, d = x.shape
    tm = _pick(m, 512)
    tn = _pick(d, 1024)
    return pl.pallas_call(
        _out_proj_kernel,
        grid=(d // tn, m // tm),
        in_specs=[pl.BlockSpec((tm, GROUP_W), lambda j, i: (i, 0)),
                  pl.BlockSpec((tm, GROUP_W), lambda j, i: (i, 0)),
                  pl.BlockSpec((tm, tn), lambda j, i: (i, j)),
                  pl.BlockSpec((2 * GROUP_W, tn), lambda j, i: (0, j))],
        out_specs=pl.BlockSpec((tm, tn), lambda j, i: (i, j)),
        out_shape=jax.ShapeDtypeStruct((m, d), F32),
        scratch_shapes=[pltpu.VMEM((2 * GROUP_W, tn), BF16)],
        compiler_params=_cparams(2),
        name="out_proj",
    )(da, ml, x, w_out)


def _ffn_kernel(*refs, sequential, tm, n_f):
    if sequential:
        (x_ref, gf_ref, gl_ref, wa_ref, wg_ref, wd_ref, cwa_ref, cwg_ref, cba_ref, cbg_ref,
         pa_ref, pg_ref, y_ref, ua_ref, ug_ref, h_sc, acc_sc, carry_sc) = refs
    else:
        (x_ref, gf_ref, gl_ref, wa_ref, wg_ref, wd_ref, cwa_ref, cwg_ref, cba_ref, cbg_ref,
         s0a_ref, s1a_ref, s0g_ref, s1g_ref, y_ref, ua_ref, ug_ref, h_sc, acc_sc) = refs
    t = pl.program_id(0)
    f = pl.program_id(1)

    @pl.when(f == 0)
    def _():
        x = x_ref[...]
        h_sc[...] = (x * lax.rsqrt(jnp.mean(x * x, axis=-1, keepdims=True) + EPS)
                     * gf_ref[...]).astype(BF16)
        acc_sc[...] = jnp.zeros_like(acc_sc)

    h = h_sc[...]

    def conv(u, which, cw_ref, cb_ref, u_out_ref):
        if sequential:
            @pl.when(t == 0)
            def _():
                carry_sc[which, f] = (pa_ref, pg_ref)[which][...]

            prev = carry_sc[which, f]
            row = lax.broadcasted_iota(jnp.int32, u.shape, 0)
            s1 = jnp.where(row == 0, prev[7:8], pltpu.roll(u, 1, 0))
            s0 = jnp.where(row == 0, prev[6:7], jnp.where(row == 1, prev[7:8], pltpu.roll(u, 2, 0)))
            tail8 = u[tm - 8:tm]
            carry_sc[which, f] = tail8
            u_out_ref[...] = tail8
        else:
            s0 = (s0a_ref, s0g_ref)[which][...]
            s1 = (s1a_ref, s1g_ref)[which][...]
            u_out_ref[...] = u
        return cb_ref[...] + cw_ref[0:1] * s0 + cw_ref[1:2] * s1 + cw_ref[2:3] * u

    ya = conv(jnp.dot(h, wa_ref[...], preferred_element_type=F32), 0, cwa_ref, cba_ref, ua_ref)
    yg = conv(jnp.dot(h, wg_ref[...], preferred_element_type=F32), 1, cwg_ref, cbg_ref, ug_ref)
    act = (ya * jax.nn.sigmoid(ya) * yg).astype(BF16)
    acc_sc[...] += jnp.dot(act, wd_ref[...], preferred_element_type=F32)

    @pl.when(f == n_f - 1)
    def _():
        x2 = x_ref[...] + acc_sc[...]
        y_ref[...] = x2 * lax.rsqrt(jnp.mean(x2 * x2, axis=-1, keepdims=True) + EPS) * gl_ref[...]


def _ffn(x, g_ffn, g_final, w_up, w_down, conv_w, conv_b, *, prefix=None, s0=None, s1=None):
    m, d = x.shape
    d_ff = w_down.shape[0]
    sequential = prefix is not None
    tm = _pick(m, 512)
    tf = _pick(d_ff, 512)
    n_f = d_ff // tf
    n_t = m // tm
    col_a = lambda r, w: pl.BlockSpec((r, w), lambda t, f: (0, f))
    col_g = lambda r, w: pl.BlockSpec((r, w), lambda t, f: (0, f + n_f))
    in_specs = [pl.BlockSpec((tm, d), lambda t, f: (t, 0)),
                pl.BlockSpec((1, d), lambda t, f: (0, 0)),
                pl.BlockSpec((1, d), lambda t, f: (0, 0)),
                col_a(d, tf), col_g(d, tf),
                pl.BlockSpec((tf, d), lambda t, f: (f, 0)),
                col_a(CONV_W, tf), col_g(CONV_W, tf), col_a(1, tf), col_g(1, tf)]
    args = [x, g_ffn, g_final, w_up, w_up, w_down, conv_w, conv_w, conv_b, conv_b]
    scratch = [pltpu.VMEM((tm, d), BF16), pltpu.VMEM((tm, d), F32)]
    if sequential:
        in_specs += [col_a(8, tf), col_g(8, tf)]
        args += [prefix, prefix]
        u_rows = 8 * n_t
        u_spec = pl.BlockSpec((8, tf), lambda t, f: (t, f))
        scratch += [pltpu.VMEM((2, n_f, 8, tf), F32)]
    else:
        rows_a = pl.BlockSpec((tm, tf), lambda t, f: (t, f))
        rows_g = pl.BlockSpec((tm, tf), lambda t, f: (t, f + n_f))
        in_specs += [rows_a, rows_a, rows_g, rows_g]
        args += [s0, s1, s0, s1]
        u_rows = m
        u_spec = rows_a
    y, ua, ug = pl.pallas_call(
        functools.partial(_ffn_kernel, sequential=sequential, tm=tm, n_f=n_f),
        grid=(n_t, n_f),
        in_specs=in_specs,
        out_specs=[pl.BlockSpec((tm, d), lambda t, f: (t, 0)), u_spec, u_spec],
        out_shape=[jax.ShapeDtypeStruct((m, d), F32),
                   jax.ShapeDtypeStruct((u_rows, d_ff), F32),
                   jax.ShapeDtypeStruct((u_rows, d_ff), F32)],
        scratch_shapes=scratch,
        compiler_params=_cparams(2),
        name="conv_ffn_seq" if sequential else "conv_ffn_rows",
    )(*args)
    return y, jnp.concatenate([ua, ug], axis=1)


def kernel(x_prompt, x_sample, cache_k, cache_v, page_table, state_C, state_n, state_m, state_conv,
           meta_tokens, g_mix, w_in, b_i, b_f, lam_q1, lam_k1, lam_q2, lam_k2, g_da, g_ml, w_out,
           g_ffn, w_up, conv_w, conv_b, w_down, g_final):
    depth = w_in.shape[0]
    assert depth == 1, "single-layer trunk"
    assert x_prompt.shape[0] == 1 and x_sample.shape[1] == 1
    seq, d = x_prompt.shape[1:]
    n_dec = x_sample.shape[0]
    assert N_META + n_dec <= TAIL_ROWS
    d_ff = w_down.shape[1]
    n_pool = cache_k.shape[1]
    gate_off = 7 * GROUP_W
    s_lo, s_hi = N_META, N_META + n_dec

    x_main = x_prompt[0]
    x_tail = jnp.concatenate([meta_tokens.astype(F32), x_sample[:, 0],
                              jnp.zeros((TAIL_ROWS - s_hi, d), F32)], axis=0)
    w_in0 = w_in[0]
    wg = jnp.zeros((d, 2 * HEAD_W), F32)
    wg = wg.at[:, :N_HEADS].set(w_in0[:, gate_off:gate_off + N_HEADS])
    wg = wg.at[:, HEAD_W:HEAD_W + N_HEADS].set(w_in0[:, gate_off + N_HEADS:gate_off + 2 * N_HEADS])
    wg = wg.astype(BF16)
    gbias = jnp.zeros((1, 2 * HEAD_W), F32)
    gbias = gbias.at[0, :N_HEADS].set(b_i[0]).at[0, HEAD_W:HEAD_W + N_HEADS].set(b_f[0])
    lam4 = jnp.stack([lam_q1[0], lam_k1[0], lam_q2[0], lam_k2[0]]).astype(F32)
    g_da2 = g_da[0].reshape(1, HEAD_W)
    g_ml2 = g_ml[0].reshape(1, GROUP_W)
    g_mix2 = g_mix[0].reshape(1, d)
    g_ffn2 = g_ffn[0].reshape(1, d)
    g_fin2 = g_final.reshape(1, d)
    w_up_b = w_up[0].astype(BF16)
    w_down_b = w_down[0].astype(BF16)
    conv_w2 = conv_w[0]
    conv_b2 = conv_b[0].reshape(1, 2 * d_ff)
    k_scale = HEAD_W ** -0.5
    q_scale = DA_QK ** -0.5

    def project(x):
        h, ig, lf = _pre(x, g_mix2, wg, gbias)
        kv_f, kv_b = _proj(h, w_in0, (1, 2), (1.0, 1.0), True)
        oth = _proj(h, w_in0, (0, 3, 4, 5, 6), (q_scale, 1.0, k_scale, 1.0, 1.0), False)
        return dict(k_f=kv_f[0], v_f=kv_f[1], k_b=kv_b[0], v_b=kv_b[1],
                    dq=oth[0], mq=oth[1], mk=oth[2], mv=oth[3], mo=oth[4], ig=ig, lf=lf)

    pm = project(x_main)
    pt = project(x_tail)

    tq = _pick(seq, 512)
    n_qt = seq // tq
    pairs = [(i, j) for i in range(n_qt) for j in range(i + 1)]
    da_main = _attn(pm["dq"], pm["k_b"], pm["v_b"], pt["k_b"], pt["v_b"], lam4, g_da2,
                    n_q_rows=seq, tq=tq, tk=tq, q_pos0=0, prefix_off=FAR, pairs=pairs)
    da_meta = _attn(pt["dq"], pm["k_b"], pm["v_b"], pt["k_b"], pt["v_b"], lam4, g_da2,
                    n_q_rows=N_META, tq=N_META, tk=tq, q_pos0=-FAR, prefix_off=0, pairs=[(0, 0)])

    c_zero = jnp.zeros((N_HEADS, HEAD_W, 2 * HEAD_W), F32)
    m_zero = jnp.zeros((1, HEAD_W), F32)
    ml_meta, c_meta, m_meta = _mlstm(pt["mq"], pt["mk"], pt["mv"], pt["mo"], pt["ig"], pt["lf"],
                                     c_zero, m_zero, g_ml2, n_rows=N_META, chunk=N_META)
    ml_main, c_fin, m_fin = _mlstm(pm["mq"], pm["mk"], pm["mv"], pm["mo"], pm["ig"], pm["lf"],
                                   c_meta, m_meta, g_ml2, n_rows=seq, chunk=_pick(seq, 256))

    f32_rows = lambda a: a[s_lo:s_hi].astype(F32)
    ck = cache_k[0].reshape(n_pool, PAGE_SIZE, GROUP_W)
    cv = cache_v[0].reshape(n_pool, PAGE_SIZE, GROUP_W)
    da_smp = _decode_attn(page_table, f32_rows(pt["dq"])[:, None, :], pt["k_f"][s_lo:s_hi, None, :],
                          pt["v_f"][s_lo:s_hi, None, :], ck, cv, lam4, g_da2)[:, 0]
    heads = lambda a: a.reshape(n_dec, N_HEADS, HEAD_W)
    m_pad = jnp.zeros((n_dec, 1, HEAD_W), F32).at[:, 0, :N_HEADS].set(state_m[0].astype(F32))
    ml_smp, c_smp, n_smp, m_smp = _mlstm_step(
        heads(f32_rows(pt["mq"])), heads(f32_rows(pt["mk"])), heads(f32_rows(pt["mv"])),
        heads(pt["mo"][s_lo:s_hi]), pt["ig"][s_lo:s_hi, None, :], pt["lf"][s_lo:s_hi, None, :],
        state_C[0].astype(F32), state_n[0].astype(F32), m_pad, g_ml2.reshape(N_HEADS, HEAD_W))
    ml_smp = ml_smp.reshape(n_dec, GROUP_W).astype(BF16)

    pad_rows = jnp.zeros((TAIL_ROWS - s_hi, GROUP_W), BF16)
    da_tail = jnp.concatenate([da_meta, da_smp, pad_rows], axis=0)
    ml_tail = jnp.concatenate([ml_meta, ml_smp, pad_rows], axis=0)
    x1_main = _out_proj(da_main, ml_main, x_main, w_out[0])
    x1_tail = _out_proj(da_tail, ml_tail, x_tail, w_out[0])

    hist = jnp.zeros((2, TAIL_ROWS, 2 * d_ff), F32).at[:, s_lo:s_hi].set(
        jnp.swapaxes(state_conv[0].astype(F32), 0, 1))
    y_tail, u_tail = _ffn(x1_tail, g_ffn2, g_fin2, w_up_b, w_down_b, conv_w2, conv_b2,
                          s0=hist[0], s1=hist[1])
    y_main, u_last = _ffn(x1_main, g_ffn2, g_fin2, w_up_b, w_down_b, conv_w2, conv_b2,
                          prefix=u_tail[N_META - 8:N_META])

    def rows5(meta_part, main_part):
        return jnp.concatenate([meta_part, main_part], axis=0).reshape(
            1, 1, N_META + seq, N_HEADS, HEAD_W)

    y_prompt = y_main[None]
    y_sample = y_tail[s_lo:s_hi, None, :]
    k_rows_p = rows5(pt["k_f"][:N_META], pm["k_f"])
    v_rows_p = rows5(pt["v_f"][:N_META], pm["v_f"])
    k_rows_s = pt["k_f"][s_lo:s_hi].reshape(1, n_dec, 1, N_HEADS, HEAD_W)
    v_rows_s = pt["v_f"][s_lo:s_hi].reshape(1, n_dec, 1, N_HEADS, HEAD_W)
    c_p = c_fin[None, None, :, :, :HEAD_W]
    n_p = c_fin[None, None, :, :, HEAD_W]
    m_p = m_fin[None, :, :N_HEADS]
    c_s = c_smp[None]
    n_s = n_smp[None]
    m_s = m_smp[None, :, 0, :N_HEADS]
    conv_p = u_last[None, None, -2:, :]
    conv_s = jnp.stack([state_conv[0][:, 1].astype(F32), u_tail[s_lo:s_hi]], axis=1)[None]
    return (y_prompt, y_sample, k_rows_p, v_rows_p, k_rows_s, v_rows_s,
            c_p, n_p, m_p, c_s, n_s, m_s, conv_p, conv_s)
```

```python
import functools

import jax
import jax.numpy as jnp
from jax import lax
from jax.experimental import pallas as pl
from jax.experimental.pallas import tpu as pltpu

F32 = jnp.float32
BF16 = jnp.bfloat16

N_META = 16
N_HEADS = 8
HEAD_W = 128
DA_QK = 64
PAGE_SIZE = 128
CONV_W = 3
EPS = 1e-6
GROUP_W = N_HEADS * HEAD_W
LAM_INIT = 0.8 - 0.6 * 1.0
TAIL_ROWS = 128
NEG_INF = float("-inf")
LOG2E = 1.4426950408889634
FAR = 1 << 24

VMEM_LIMIT = 52 * 1024 * 1024

_NT = (((1,), (1,)), ((), ()))


def _cparams(n_axes):
    return pltpu.CompilerParams(dimension_semantics=("arbitrary",) * n_axes,
                                vmem_limit_bytes=VMEM_LIMIT)


def _pick(total, pref):
    t = min(pref, total)
    while total % t:
        t //= 2
    return t


def _pre_kernel(x_ref, g_ref, wg_ref, bias_ref, h_ref, ig_ref, lf_ref):
    x = x_ref[...]
    y = x * lax.rsqrt(jnp.mean(x * x, axis=-1, keepdims=True) + EPS) * g_ref[...]
    hb = y.astype(BF16)
    h_ref[...] = hb
    z = jnp.dot(hb, wg_ref[...], preferred_element_type=F32) + bias_ref[...]
    ig_ref[...] = z[:, :HEAD_W]
    zf = z[:, HEAD_W:]
    lf_ref[...] = jnp.minimum(zf, 0.0) - jnp.log1p(jnp.exp(-jnp.abs(zf)))


def _pre(x, g, wg, bias):
    m, d = x.shape
    tm = _pick(m, 512)
    return pl.pallas_call(
        _pre_kernel,
        grid=(m // tm,),
        in_specs=[pl.BlockSpec((tm, d), lambda i: (i, 0)),
                  pl.BlockSpec((1, d), lambda i: (0, 0)),
                  pl.BlockSpec((d, 2 * HEAD_W), lambda i: (0, 0)),
                  pl.BlockSpec((1, 2 * HEAD_W), lambda i: (0, 0))],
        out_specs=[pl.BlockSpec((tm, d), lambda i: (i, 0)),
                   pl.BlockSpec((tm, HEAD_W), lambda i: (i, 0)),
                   pl.BlockSpec((tm, HEAD_W), lambda i: (i, 0))],
        out_shape=[jax.ShapeDtypeStruct((m, d), BF16),
                   jax.ShapeDtypeStruct((m, HEAD_W), F32),
                   jax.ShapeDtypeStruct((m, HEAD_W), F32)],
        compiler_params=_cparams(1),
        name="pre_norm_gates",
    )(x, g, wg, bias)


def _proj_kernel(h_ref, w_ref, s_ref, *rest, with_f32):
    if with_f32:
        of_ref, ob_ref, wb_ref = rest
    else:
        ob_ref, wb_ref = rest

    @pl.when(pl.program_id(1) == 0)
    def _():
        wb_ref[...] = w_ref[...].astype(BF16)

    z = jnp.dot(h_ref[...], wb_ref[...], preferred_element_type=F32)
    if with_f32:
        of_ref[0] = z
    ob_ref[0] = (z * s_ref[0, :, 0:1]).astype(BF16)


def _proj(h, w_in, groups, scales, with_f32):
    m, d = h.shape
    tm = _pick(m, 512)
    ng = len(groups)
    first, rest_off = groups[0], groups[1] - 1 if ng > 1 else 0
    assert list(groups) == [first] + [rest_off + k for k in range(1, ng)]
    s_arr = jnp.broadcast_to(jnp.asarray(scales, F32)[:, None, None], (ng, 1, HEAD_W))

    def w_map(j, i):
        return (0, jnp.where(j == 0, first, j + rest_off))

    out_specs = [pl.BlockSpec((1, tm, GROUP_W), lambda j, i: (j, i, 0))]
    out_shape = [jax.ShapeDtypeStruct((ng, m, GROUP_W), BF16)]
    if with_f32:
        out_specs = [pl.BlockSpec((1, tm, GROUP_W), lambda j, i: (j, i, 0))] + out_specs
        out_shape = [jax.ShapeDtypeStruct((ng, m, GROUP_W), F32)] + out_shape
    outs = pl.pallas_call(
        functools.partial(_proj_kernel, with_f32=with_f32),
        grid=(ng, m // tm),
        in_specs=[pl.BlockSpec((tm, d), lambda j, i: (i, 0)),
                  pl.BlockSpec((d, GROUP_W), w_map),
                  pl.BlockSpec((1, 1, HEAD_W), lambda j, i: (j, 0, 0))],
        out_specs=out_specs,
        out_shape=out_shape,
        scratch_shapes=[pltpu.VMEM((d, GROUP_W), BF16)],
        compiler_params=_cparams(2),
        name="in_proj_f32" if with_f32 else "in_proj",
    )(h, w_in, s_arr)
    return outs if with_f32 else outs[0]


def _attn_kernel(qi_ref, kj_ref, q_ref, k_ref, v_ref, kp_ref, vp_ref, lam_ref, g_ref,
                 o_ref, m_sc, l_sc, acc_sc, *, tq, tk, sub, ahead, q_pos0, prefix_off, n_steps):
    step = pl.program_id(1)
    i = qi_ref[step]
    j = kj_ref[step]
    q = q_ref[...]
    lane = lax.broadcasted_iota(jnp.int32, q.shape, 1)
    zero = jnp.zeros_like(q)
    q2 = jnp.concatenate([jnp.where(lane < DA_QK, q, zero), jnp.where(lane >= DA_QK, q, zero)], axis=0)
    n_sub = 2 * tq // sub

    @pl.when(j == 0)
    def _():
        sp = lax.dot_general(q2, kp_ref[...], _NT, preferred_element_type=F32)
        col = lax.broadcasted_iota(jnp.int32, sp.shape, 1)
        r = lax.broadcasted_iota(jnp.int32, sp.shape, 0)
        ok = (col < N_META) & (col <= jnp.where(r >= tq, r - tq, r) + prefix_off)
        sp = jnp.where(ok, sp, NEG_INF)
        m0 = jnp.max(sp, axis=1, keepdims=True)
        p = jnp.exp2(sp - m0)
        m_sc[...] = jnp.broadcast_to(m0, m_sc.shape)
        l_sc[...] = jnp.broadcast_to(jnp.sum(p, axis=1, keepdims=True), l_sc.shape)
        acc_sc[...] = jnp.dot(p.astype(BF16), vp_ref[...], preferred_element_type=F32)

    def block(masked):
        k = k_ref[...]
        v = v_ref[...]
        if masked:
            rel = (lax.broadcasted_iota(jnp.int32, (sub, tk), 1)
                   - lax.broadcasted_iota(jnp.int32, (sub, tk), 0))
        m_all, l_all, acc_all = m_sc[...], l_sc[...], acc_sc[...]
        new = []

        def scores(r):
            return lax.dot_general(q2[r * sub:(r + 1) * sub], k, _NT, preferred_element_type=F32)

        pending = [scores(r) for r in range(min(ahead, n_sub))]
        for r in range(n_sub):
            rows = slice(r * sub, (r + 1) * sub)
            s = pending.pop(0)
            if r + ahead < n_sub:
                pending.append(scores(r + ahead))
            if masked:
                s = jnp.where(rel <= q_pos0 + i * tq + (r * sub) % tq - j * tk, s, NEG_INF)
            m_prev = m_all[rows]
            m_new = jnp.maximum(m_prev, jnp.max(s, axis=1, keepdims=True))
            alpha = jnp.exp2(m_prev - m_new)
            p = jnp.exp2(s - pltpu.repeat(m_new, tk // HEAD_W, axis=1))
            new.append((m_new, alpha * l_all[rows] + jnp.sum(p, axis=1, keepdims=True),
                        alpha * acc_all[rows] + jnp.dot(p.astype(BF16), v, preferred_element_type=F32)))
        for r, (m_new, l_new, acc_new) in enumerate(new):
            rows = slice(r * sub, (r + 1) * sub)
            m_sc[rows] = m_new
            l_sc[rows] = l_new
            acc_sc[rows] = acc_new

    needs_mask = j * tk + (tk - 1) > q_pos0 + i * tq
    pl.when(needs_mask)(lambda: block(True))
    pl.when(jnp.logical_not(needs_mask))(lambda: block(False))

    is_last = jnp.logical_or(step == n_steps - 1, qi_ref[step + 1] != i)

    @pl.when(is_last)
    def _():
        o = acc_sc[...] / l_sc[...]
        lam = (jnp.exp(jnp.sum(lam_ref[0:1, :] * lam_ref[1:2, :], axis=1, keepdims=True))
               - jnp.exp(jnp.sum(lam_ref[2:3, :] * lam_ref[3:4, :], axis=1, keepdims=True)) + LAM_INIT)
        d = o[:tq] - lam * o[tq:]
        y = d * lax.rsqrt(jnp.mean(d * d, axis=-1, keepdims=True) + EPS) * g_ref[...]
        o_ref[...] = (y * (1.0 - LAM_INIT)).astype(o_ref.dtype)


def _attn(q, k, v, kp, vp, lam4, g_da, *, n_q_rows, tq, tk, q_pos0, prefix_off, pairs):
    qi = jnp.asarray([p[0] for p in pairs] + [-1], jnp.int32)
    kj = jnp.asarray([p[1] for p in pairs] + [0], jnp.int32)
    n_steps = len(pairs)
    kern = functools.partial(_attn_kernel, tq=tq, tk=tk, sub=min(256, 2 * tq), ahead=2, q_pos0=q_pos0,
                             prefix_off=prefix_off, n_steps=n_steps)
    grid_spec = pltpu.PrefetchScalarGridSpec(
        num_scalar_prefetch=2,
        grid=(N_HEADS, n_steps),
        in_specs=[pl.BlockSpec((tq, HEAD_W), lambda h, s, qi, kj: (qi[s], h)),
                  pl.BlockSpec((tk, HEAD_W), lambda h, s, qi, kj: (kj[s], h)),
                  pl.BlockSpec((tk, HEAD_W), lambda h, s, qi, kj: (kj[s], h)),
                  pl.BlockSpec((TAIL_ROWS, HEAD_W), lambda h, s, qi, kj: (0, h)),
                  pl.BlockSpec((TAIL_ROWS, HEAD_W), lambda h, s, qi, kj: (0, h)),
                  pl.BlockSpec((4, DA_QK), lambda h, s, qi, kj: (0, 0)),
                  pl.BlockSpec((1, HEAD_W), lambda h, s, qi, kj: (0, 0))],
        out_specs=pl.BlockSpec((tq, HEAD_W), lambda h, s, qi, kj: (qi[s], h)),
        scratch_shapes=[pltpu.VMEM((2 * tq, HEAD_W), F32), pltpu.VMEM((2 * tq, HEAD_W), F32),
                        pltpu.VMEM((2 * tq, HEAD_W), F32)])
    return pl.pallas_call(
        kern, grid_spec=grid_spec,
        out_shape=jax.ShapeDtypeStruct((n_q_rows, GROUP_W), BF16),
        compiler_params=_cparams(2),
        name="diff_attn_prompt",
    )(qi, kj, q, k, v, kp, vp, lam4, g_da)


def _mlstm_kernel(q_ref, k_ref, v_ref, o_ref, ig_ref, lf_ref, c0_ref, m0_ref, g_ref,
                  y_ref, c_out_ref, m_out_ref, c_sc, m_sc, *, chunk, n_chunks):
    c = pl.program_id(0)

    @pl.when(c == 0)
    def _():
        c_sc[...] = c0_ref[...]
        m_sc[...] = m0_ref[...]

    L = chunk
    ig = ig_ref[...]
    lf = lf_ref[...]
    row = lax.broadcasted_iota(jnp.int32, (L, L), 0)
    col = lax.broadcasted_iota(jnp.int32, (L, L), 1)
    causal = col <= row
    b_col = jnp.dot(causal.astype(F32), lf, preferred_element_type=F32,
                    precision=lax.Precision.HIGHEST)
    b_row = b_col.T
    ig_row = ig.T
    m_prev_all = m_sc[...]
    lane1 = lax.broadcasted_iota(jnp.int32, (1, HEAD_W), 1)
    m_next_all = m_prev_all
    ones = jnp.ones((L, HEAD_W), BF16)

    for h in range(N_HEADS):
        hs = slice(h * HEAD_W, (h + 1) * HEAD_W)
        bc = b_col[:, h:h + 1]
        m_prev = m_prev_all[:, h:h + 1]
        dm = jnp.where(causal, bc - b_row[h:h + 1, :] + ig_row[h:h + 1, :], NEG_INF)
        inter = bc + m_prev
        m_t = jnp.maximum(inter, jnp.max(dm, axis=1, keepdims=True))
        w = jnp.exp(dm - m_t)
        g = jnp.exp(inter - m_t)
        qh = q_ref[:, hs]
        kh = k_ref[:, hs]
        v_aug = jnp.concatenate([v_ref[:, hs], ones], axis=1)
        s = lax.dot_general(qh, kh, _NT, preferred_element_type=F32)
        a = (s * w).astype(BF16)
        c_h = c_sc[h]
        num_aug = (jnp.dot(a, v_aug, preferred_element_type=F32)
                   + g * jnp.dot(qh, c_h.astype(BF16), preferred_element_type=F32))
        num = num_aug[:, :HEAD_W]
        den = num_aug[:, HEAD_W:]
        hh = num / jnp.maximum(jnp.abs(den), jnp.exp(-m_t))
        y = hh * lax.rsqrt(jnp.mean(hh * hh, axis=-1, keepdims=True) + EPS) * g_ref[:, hs]
        y_ref[:, hs] = (y * jax.nn.sigmoid(o_ref[:, hs].astype(F32))).astype(y_ref.dtype)
        b_last = bc[L - 1:L]
        m_new = m_t[L - 1:L]
        ws = jnp.exp(b_last - bc + ig[:, h:h + 1] - m_new)
        decay = jnp.exp(b_last + m_prev - m_new)
        kw_t = (kh.astype(F32) * ws).T.astype(BF16)
        c_sc[h] = decay * c_h + jnp.dot(kw_t, v_aug, preferred_element_type=F32)
        m_next_all = jnp.where(lane1 == h, m_new, m_next_all)

    m_sc[...] = m_next_all

    @pl.when(c == n_chunks - 1)
    def _():
        c_out_ref[...] = c_sc[...]
        m_out_ref[...] = m_sc[...]


def _mlstm(q, k, v, o, ig, lf, c0, m0, g_ml, *, n_rows, chunk):
    n_chunks = n_rows // chunk
    row_blk = lambda w: pl.BlockSpec((chunk, w), lambda c: (c, 0))
    whole = lambda shape: pl.BlockSpec(shape, lambda c: (0,) * len(shape))
    return pl.pallas_call(
        functools.partial(_mlstm_kernel, chunk=chunk, n_chunks=n_chunks),
        grid=(n_chunks,),
        in_specs=[row_blk(GROUP_W), row_blk(GROUP_W), row_blk(GROUP_W), row_blk(GROUP_W),
                  row_blk(HEAD_W), row_blk(HEAD_W),
                  whole((N_HEADS, HEAD_W, 2 * HEAD_W)), whole((1, HEAD_W)), whole((1, GROUP_W))],
        out_specs=[row_blk(GROUP_W), whole((N_HEADS, HEAD_W, 2 * HEAD_W)), whole((1, HEAD_W))],
        out_shape=[jax.ShapeDtypeStruct((n_rows, GROUP_W), BF16),
                   jax.ShapeDtypeStruct((N_HEADS, HEAD_W, 2 * HEAD_W), F32),
                   jax.ShapeDtypeStruct((1, HEAD_W), F32)],
        scratch_shapes=[pltpu.VMEM((N_HEADS, HEAD_W, 2 * HEAD_W), F32), pltpu.VMEM((1, HEAD_W), F32)],
        compiler_params=_cparams(1),
        name="mlstm_prompt",
    )(q, k, v, o, ig, lf, c0, m0, g_ml)


def _decode_kernel(pt_ref, q_ref, kn_ref, vn_ref, lam_ref, g_ref, *rest, pages_per_step, n_steps):
    k_refs = rest[:pages_per_step]
    v_refs = rest[pages_per_step:2 * pages_per_step]
    o_ref, m_sc, l_sc, acc_sc = rest[2 * pages_per_step:]
    j = pl.program_id(1)
    n_rows = 2 * N_HEADS
    page_rows = PAGE_SIZE * N_HEADS
    q16 = q_ref[0]
    rr = lax.broadcasted_iota(jnp.int32, (n_rows, HEAD_W), 0)
    cc = lax.broadcasted_iota(jnp.int32, (n_rows, HEAD_W), 1)
    qm = jnp.where(cc // DA_QK == rr // N_HEADS, q16, 0.0)

    @pl.when(j == 0)
    def _():
        m_sc[...] = jnp.broadcast_to(jnp.sum(qm * kn_ref[0], axis=1, keepdims=True), m_sc.shape)
        l_sc[...] = jnp.ones_like(l_sc)
        acc_sc[...] = vn_ref[0]

    qb = qm.astype(BF16)
    key_head = lax.broadcasted_iota(jnp.int32, (n_rows, page_rows), 1) % N_HEADS
    row_head = lax.broadcasted_iota(jnp.int32, (n_rows, page_rows), 0) % N_HEADS
    own = key_head == row_head
    s_pages = [jnp.where(own, lax.dot_general(qb, kr[0].astype(BF16), _NT, preferred_element_type=F32),
                         NEG_INF) for kr in k_refs]
    s_max = s_pages[0]
    for s in s_pages[1:]:
        s_max = jnp.maximum(s_max, s)
    m_prev = m_sc[...]
    m_new = jnp.maximum(m_prev, jnp.max(s_max, axis=1, keepdims=True))
    alpha = jnp.exp2(m_prev - m_new)
    m_rep = pltpu.repeat(m_new, page_rows // HEAD_W, axis=1)
    l_add = jnp.zeros((n_rows, 1), F32)
    pv = jnp.zeros((n_rows, HEAD_W), F32)
    for s, vr in zip(s_pages, v_refs):
        p = jnp.exp2(s - m_rep)
        l_add += jnp.sum(p, axis=1, keepdims=True)
        pv += jnp.dot(p.astype(BF16), vr[0].astype(BF16), preferred_element_type=F32)
    l_sc[...] = alpha * l_sc[...] + l_add
    acc_sc[...] = alpha * acc_sc[...] + pv
    m_sc[...] = m_new

    @pl.when(j == n_steps - 1)
    def _():
        o = acc_sc[...] / l_sc[...]
        lam = (jnp.exp(jnp.sum(lam_ref[0:1, :] * lam_ref[1:2, :], axis=1, keepdims=True))
               - jnp.exp(jnp.sum(lam_ref[2:3, :] * lam_ref[3:4, :], axis=1, keepdims=True)) + LAM_INIT)
        d = o[:N_HEADS] - lam * o[N_HEADS:]
        y = d * lax.rsqrt(jnp.mean(d * d, axis=-1, keepdims=True) + EPS) * g_ref[...]
        o_ref[0] = y * (1.0 - LAM_INIT)


def _decode_attn(page_table, q, k_new, v_new, cache_k, cache_v, lam4, g_da):
    b, n_pages = page_table.shape
    pps = _pick(n_pages, 8)
    n_steps = n_pages // pps
    pt_flat = page_table.reshape(-1)
    n_rows = 2 * N_HEADS

    def page_spec(g):
        return pl.BlockSpec((1, PAGE_SIZE * N_HEADS, HEAD_W),
                            lambda bi, j, pt: (pt[bi * n_pages + j * pps + g], 0, 0))

    seq_spec = pl.BlockSpec((1, n_rows, HEAD_W), lambda bi, j, pt: (bi, 0, 0))
    grid_spec = pltpu.PrefetchScalarGridSpec(
        num_scalar_prefetch=1,
        grid=(b, n_steps),
        in_specs=[seq_spec, seq_spec, seq_spec,
                  pl.BlockSpec((4, DA_QK), lambda bi, j, pt: (0, 0)),
                  pl.BlockSpec((1, HEAD_W), lambda bi, j, pt: (0, 0))]
                 + [page_spec(g) for g in range(pps)] + [page_spec(g) for g in range(pps)],
        out_specs=pl.BlockSpec((1, N_HEADS, HEAD_W), lambda bi, j, pt: (bi, 0, 0)),
        scratch_shapes=[pltpu.VMEM((n_rows, HEAD_W), F32), pltpu.VMEM((n_rows, HEAD_W), F32),
                        pltpu.VMEM((n_rows, HEAD_W), F32)])
    return pl.pallas_call(
        functools.partial(_decode_kernel, pages_per_step=pps, n_steps=n_steps),
        grid_spec=grid_spec,
        out_shape=jax.ShapeDtypeStruct((b, N_HEADS, HEAD_W), F32),
        compiler_params=_cparams(2),
        name="diff_attn_decode",
    )(pt_flat, q, k_new, v_new, lam4, g_da, *([cache_k] * pps), *([cache_v] * pps))


def _mlstm_step_kernel(q_ref, k_ref, v_ref, o_ref, ig_ref, lf_ref, c_ref, n_ref, m_ref, g_ref,
                       y_ref, c_out_ref, n_out_ref, m_out_ref):
    q8 = q_ref[0]
    k8 = k_ref[0]
    v8 = v_ref[0]
    n8 = n_ref[0]
    ig = ig_ref[0]
    lf = lf_ref[0]
    m = m_ref[0]
    inter = lf + m
    m_t = jnp.maximum(inter, ig)
    w_all = jnp.exp(ig - m_t)
    g_all = jnp.exp(inter - m_t)
    floor_all = jnp.exp(-m_t)
    m_out_ref[0] = m_t
    q_t = q8.T
    k_t = k8.T
    qk = jnp.sum(q8 * k8, axis=1, keepdims=True)
    qn = jnp.sum(q8 * n8, axis=1, keepdims=True)
    for h in range(N_HEADS):
        w = w_all[:, h:h + 1]
        g = g_all[:, h:h + 1]
        c_h = c_ref[0, h]
        vh = v8[h:h + 1]
        a = qk[h:h + 1] * w
        q_c = jnp.sum(q_t[:, h:h + 1] * c_h, axis=0, keepdims=True)
        num = a * vh + g * q_c
        den = a + g * qn[h:h + 1]
        hh = num / jnp.maximum(jnp.abs(den), floor_all[:, h:h + 1])
        y = hh * lax.rsqrt(jnp.mean(hh * hh, axis=-1, keepdims=True) + EPS) * g_ref[h:h + 1]
        y_ref[0, h:h + 1] = (y * jax.nn.sigmoid(o_ref[0, h:h + 1].astype(F32))).astype(y_ref.dtype)
        c_out_ref[0, h] = g * c_h + w * (k_t[:, h:h + 1] * vh)
        n_out_ref[0, h:h + 1] = g * n8[h:h + 1] + w * k8[h:h + 1]


def _mlstm_step(q, k, v, o, ig, lf, c, n, m, g_ml):
    b = q.shape[0]
    hv = pl.BlockSpec((1, N_HEADS, HEAD_W), lambda i: (i, 0, 0))
    gate = pl.BlockSpec((1, 1, HEAD_W), lambda i: (i, 0, 0))
    cs = pl.BlockSpec((1, N_HEADS, HEAD_W, HEAD_W), lambda i: (i, 0, 0, 0))
    return pl.pallas_call(
        _mlstm_step_kernel,
        grid=(b,),
        in_specs=[hv, hv, hv, hv, gate, gate, cs, hv, gate,
                  pl.BlockSpec((N_HEADS, HEAD_W), lambda i: (0, 0))],
        out_specs=[hv, cs, hv, gate],
        out_shape=[jax.ShapeDtypeStruct((b, N_HEADS, HEAD_W), F32),
                   jax.ShapeDtypeStruct(c.shape, F32),
                   jax.ShapeDtypeStruct((b, N_HEADS, HEAD_W), F32),
                   jax.ShapeDtypeStruct((b, 1, HEAD_W), F32)],
        compiler_params=_cparams(1),
        name="mlstm_sample",
    )(q, k, v, o, ig, lf, c, n, m, g_ml)


def _out_proj_kernel(da_ref, ml_ref, x_ref, w_ref, o_ref, wb_ref):
    @pl.when(pl.program_id(1) == 0)
    def _():
        wb_ref[...] = w_ref[...].astype(BF16)

    y = jnp.concatenate([da_ref[...], ml_ref[...]], axis=1)
    o_ref[...] = x_ref[...] + jnp.dot(y, wb_ref[...], preferred_element_type=F32)


def _out_proj(da, ml, x, w_out):
    m, d = x.shape
    tm = _pick(m, 512)
    tn = _pick(d, 1024)
    return pl.pallas_call(
        _out_proj_kernel,
        grid=(d // tn, m // tm),
        in_specs=[pl.BlockSpec((tm, GROUP_W), lambda j, i: (i, 0)),
                  pl.BlockSpec((tm, GROUP_W), lambda j, i: (i, 0)),
                  pl.BlockSpec((tm, tn), lambda j, i: (i, j)),
                  pl.BlockSpec((2 * GROUP_W, tn), lambda j, i: (0, j))],
        out_specs=pl.BlockSpec((tm, tn), lambda j, i: (i, j)),
        out_shape=jax.ShapeDtypeStruct((m, d), F32),
        scratch_shapes=[pltpu.VMEM((2 * GROUP_W, tn), BF16)],
        compiler_params=_cparams(2),
        name="out_proj",
    )(da, ml, x, w_out)


def _ffn_kernel(*refs, sequential, tm, n_f):
    if sequential:
        (x_ref, gf_ref, gl_ref, wa_ref, wg_ref, wd_ref, cwa_ref, cwg_ref, cba_ref, cbg_ref,
         pa_ref, pg_ref, y_ref, ua_ref, ug_ref, h_sc, acc_sc, carry_sc) = refs
    else:
        (x_ref, gf_ref, gl_ref, wa_ref, wg_ref, wd_ref, cwa_ref, cwg_ref, cba_ref, cbg_ref,
         s0a_ref, s1a_ref, s0g_ref, s1g_ref, y_ref, ua_ref, ug_ref, h_sc, acc_sc) = refs
    t = pl.program_id(0)
    f = pl.program_id(1)

    @pl.when(f == 0)
    def _():
        x = x_ref[...]
        h_sc[...] = (x * lax.rsqrt(jnp.mean(x * x, axis=-1, keepdims=True) + EPS)
                     * gf_ref[...]).astype(BF16)
        acc_sc[...] = jnp.zeros_like(acc_sc)

    h = h_sc[...]

    def conv(u, which, cw_ref, cb_ref, u_out_ref):
        if sequential:
            @pl.when(t == 0)
            def _():
                carry_sc[which, f] = (pa_ref, pg_ref)[which][...]

            prev = carry_sc[which, f]
            row = lax.broadcasted_iota(jnp.int32, u.shape, 0)
            s1 = jnp.where(row == 0, prev[7:8], pltpu.roll(u, 1, 0))
            s0 = jnp.where(row == 0, prev[6:7], jnp.where(row == 1, prev[7:8], pltpu.roll(u, 2, 0)))
            tail8 = u[tm - 8:tm]
            carry_sc[which, f] = tail8
            u_out_ref[...] = tail8
        else:
            s0 = (s0a_ref, s0g_ref)[which][...]
            s1 = (s1a_ref, s1g_ref)[which][...]
            u_out_ref[...] = u
        return cb_ref[...] + cw_ref[0:1] * s0 + cw_ref[1:2] * s1 + cw_ref[2:3] * u

    ya = conv(jnp.dot(h, wa_ref[...], preferred_element_type=F32), 0, cwa_ref, cba_ref, ua_ref)
    yg = conv(jnp.dot(h, wg_ref[...], preferred_element_type=F32), 1, cwg_ref, cbg_ref, ug_ref)
    act = (ya * jax.nn.sigmoid(ya) * yg).astype(BF16)
    acc_sc[...] += jnp.dot(act, wd_ref[...], preferred_element_type=F32)

    @pl.when(f == n_f - 1)
    def _():
        x2 = x_ref[...] + acc_sc[...]
        y_ref[...] = x2 * lax.rsqrt(jnp.mean(x2 * x2, axis=-1, keepdims=True) + EPS) * gl_ref[...]


def _ffn(x, g_ffn, g_final, w_up, w_down, conv_w, conv_b, *, prefix=None, s0=None, s1=None):
    m, d = x.shape
    d_ff = w_down.shape[0]
    sequential = prefix is not None
    tm = _pick(m, 512)
    tf = _pick(d_ff, 512)
    n_f = d_ff // tf
    n_t = m // tm
    col_a = lambda r, w: pl.BlockSpec((r, w), lambda t, f: (0, f))
    col_g = lambda r, w: pl.BlockSpec((r, w), lambda t, f: (0, f + n_f))
    in_specs = [pl.BlockSpec((tm, d), lambda t, f: (t, 0)),
                pl.BlockSpec((1, d), lambda t, f: (0, 0)),
                pl.BlockSpec((1, d), lambda t, f: (0, 0)),
                col_a(d, tf), col_g(d, tf),
                pl.BlockSpec((tf, d), lambda t, f: (f, 0)),
                col_a(CONV_W, tf), col_g(CONV_W, tf), col_a(1, tf), col_g(1, tf)]
    args = [x, g_ffn, g_final, w_up, w_up, w_down, conv_w, conv_w, conv_b, conv_b]
    scratch = [pltpu.VMEM((tm, d), BF16), pltpu.VMEM((tm, d), F32)]
    if sequential:
        in_specs += [col_a(8, tf), col_g(8, tf)]
        args += [prefix, prefix]
        u_rows = 8 * n_t
        u_spec = pl.BlockSpec((8, tf), lambda t, f: (t, f))
        scratch += [pltpu.VMEM((2, n_f, 8, tf), F32)]
    else:
        rows_a = pl.BlockSpec((tm, tf), lambda t, f: (t, f))
        rows_g = pl.BlockSpec((tm, tf), lambda t, f: (t, f + n_f))
        in_specs += [rows_a, rows_a, rows_g, rows_g]
        args += [s0, s1, s0, s1]
        u_rows = m
        u_spec = rows_a
    y, ua, ug = pl.pallas_call(
        functools.partial(_ffn_kernel, sequential=sequential, tm=tm, n_f=n_f),
        grid=(n_t, n_f),
        in_specs=in_specs,
        out_specs=[pl.BlockSpec((tm, d), lambda t, f: (t, 0)), u_spec, u_spec],
        out_shape=[jax.ShapeDtypeStruct((m, d), F32),
                   jax.ShapeDtypeStruct((u_rows, d_ff), F32),
                   jax.ShapeDtypeStruct((u_rows, d_ff), F32)],
        scratch_shapes=scratch,
        compiler_params=_cparams(2),
        name="conv_ffn_seq" if sequential else "conv_ffn_rows",
    )(*args)
    return y, jnp.concatenate([ua, ug], axis=1)


def kernel(x_prompt, x_sample, cache_k, cache_v, page_table, state_C, state_n, state_m, state_conv,
           meta_tokens, g_mix, w_in, b_i, b_f, lam_q1, lam_k1, lam_q2, lam_k2, g_da, g_ml, w_out,
           g_ffn, w_up, conv_w, conv_b, w_down, g_final):
    depth = w_in.shape[0]
    assert depth == 1, "single-layer trunk"
    assert x_prompt.shape[0] == 1 and x_sample.shape[1] == 1
    seq, d = x_prompt.shape[1:]
    n_dec = x_sample.shape[0]
    assert N_META + n_dec <= TAIL_ROWS
    d_ff = w_down.shape[1]
    n_pool = cache_k.shape[1]
    gate_off = 7 * GROUP_W
    s_lo, s_hi = N_META, N_META + n_dec

    x_main = x_prompt[0]
    x_tail = jnp.concatenate([meta_tokens.astype(F32), x_sample[:, 0],
                              jnp.zeros((TAIL_ROWS - s_hi, d), F32)], axis=0)
    w_in0 = w_in[0]
    wg = jnp.zeros((d, 2 * HEAD_W), F32)
    wg = wg.at[:, :N_HEADS].set(w_in0[:, gate_off:gate_off + N_HEADS])
    wg = wg.at[:, HEAD_W:HEAD_W + N_HEADS].set(w_in0[:, gate_off + N_HEADS:gate_off + 2 * N_HEADS])
    wg = wg.astype(BF16)
    gbias = jnp.zeros((1, 2 * HEAD_W), F32)
    gbias = gbias.at[0, :N_HEADS].set(b_i[0]).at[0, HEAD_W:HEAD_W + N_HEADS].set(b_f[0])
    lam4 = jnp.stack([lam_q1[0], lam_k1[0], lam_q2[0], lam_k2[0]]).astype(F32)
    g_da2 = g_da[0].reshape(1, HEAD_W)
    g_ml2 = g_ml[0].reshape(1, GROUP_W)
    g_mix2 = g_mix[0].reshape(1, d)
    g_ffn2 = g_ffn[0].reshape(1, d)
    g_fin2 = g_final.reshape(1, d)
    w_up_b = w_up[0].astype(BF16)
    w_down_b = w_down[0].astype(BF16)
    conv_w2 = conv_w[0]
    conv_b2 = conv_b[0].reshape(1, 2 * d_ff)
    k_scale = HEAD_W ** -0.5
    q_scale = DA_QK ** -0.5 * LOG2E

    def project(x):
        h, ig, lf = _pre(x, g_mix2, wg, gbias)
        kv_f, kv_b = _proj(h, w_in0, (1, 2), (1.0, 1.0), True)
        oth = _proj(h, w_in0, (0, 3, 4, 5, 6), (q_scale, 1.0, k_scale, 1.0, 1.0), False)
        return dict(k_f=kv_f[0], v_f=kv_f[1], k_b=kv_b[0], v_b=kv_b[1],
                    dq=oth[0], mq=oth[1], mk=oth[2], mv=oth[3], mo=oth[4], ig=ig, lf=lf)

    pm = project(x_main)
    pt = project(x_tail)

    tq = _pick(seq, 512)
    n_qt = seq // tq
    pairs = [(i, j) for i in range(n_qt) for j in range(i + 1)]
    da_main = _attn(pm["dq"], pm["k_b"], pm["v_b"], pt["k_b"], pt["v_b"], lam4, g_da2,
                    n_q_rows=seq, tq=tq, tk=tq, q_pos0=0, prefix_off=FAR, pairs=pairs)
    da_meta = _attn(pt["dq"], pm["k_b"], pm["v_b"], pt["k_b"], pt["v_b"], lam4, g_da2,
                    n_q_rows=N_META, tq=N_META, tk=tq, q_pos0=-FAR, prefix_off=0, pairs=[(0, 0)])

    c_zero = jnp.zeros((N_HEADS, HEAD_W, 2 * HEAD_W), F32)
    m_zero = jnp.zeros((1, HEAD_W), F32)
    ml_meta, c_meta, m_meta = _mlstm(pt["mq"], pt["mk"], pt["mv"], pt["mo"], pt["ig"], pt["lf"],
                                     c_zero, m_zero, g_ml2, n_rows=N_META, chunk=N_META)
    ml_main, c_fin, m_fin = _mlstm(pm["mq"], pm["mk"], pm["mv"], pm["mo"], pm["ig"], pm["lf"],
                                   c_meta, m_meta, g_ml2, n_rows=seq, chunk=_pick(seq, 256))

    f32_rows = lambda a: a[s_lo:s_hi].astype(F32)
    heads = lambda a: a.reshape(n_dec, N_HEADS, HEAD_W)
    twice = lambda a: jnp.concatenate([heads(a), heads(a)], axis=1)
    ck = cache_k[0].reshape(n_pool, PAGE_SIZE * N_HEADS, HEAD_W)
    cv = cache_v[0].reshape(n_pool, PAGE_SIZE * N_HEADS, HEAD_W)
    da_smp = _decode_attn(page_table, twice(f32_rows(pt["dq"])), twice(pt["k_f"][s_lo:s_hi]),
                          twice(pt["v_f"][s_lo:s_hi]), ck, cv, lam4, g_da2)
    da_smp = da_smp.reshape(n_dec, GROUP_W).astype(BF16)
    m_pad = jnp.zeros((n_dec, 1, HEAD_W), F32).at[:, 0, :N_HEADS].set(state_m[0].astype(F32))
    ml_smp, c_smp, n_smp, m_smp = _mlstm_step(
        heads(f32_rows(pt["mq"])), heads(f32_rows(pt["mk"])), heads(f32_rows(pt["mv"])),
        heads(pt["mo"][s_lo:s_hi]), pt["ig"][s_lo:s_hi, None, :], pt["lf"][s_lo:s_hi, None, :],
        state_C[0].astype(F32), state_n[0].astype(F32), m_pad, g_ml2.reshape(N_HEADS, HEAD_W))
    ml_smp = ml_smp.reshape(n_dec, GROUP_W).astype(BF16)

    pad_rows = jnp.zeros((TAIL_ROWS - s_hi, GROUP_W), BF16)
    da_tail = jnp.concatenate([da_meta, da_smp, pad_rows], axis=0)
    ml_tail = jnp.concatenate([ml_meta, ml_smp, pad_rows], axis=0)
    x1_main = _out_proj(da_main, ml_main, x_main, w_out[0])
    x1_tail = _out_proj(da_tail, ml_tail, x_tail, w_out[0])

    hist = jnp.zeros((2, TAIL_ROWS, 2 * d_ff), F32).at[:, s_lo:s_hi].set(
        jnp.swapaxes(state_conv[0].astype(F32), 0, 1))
    y_tail, u_tail = _ffn(x1_tail, g_ffn2, g_fin2, w_up_b, w_down_b, conv_w2, conv_b2,
                          s0=hist[0], s1=hist[1])
    y_main, u_last = _ffn(x1_main, g_ffn2, g_fin2, w_up_b, w_down_b, conv_w2, conv_b2,
                          prefix=u_tail[N_META - 8:N_META])

    def rows5(meta_part, main_part):
        return jnp.concatenate([meta_part, main_part], axis=0).reshape(
            1, 1, N_META + seq, N_HEADS, HEAD_W)

    y_prompt = y_main[None]
    y_sample = y_tail[s_lo:s_hi, None, :]
    k_rows_p = rows5(pt["k_f"][:N_META], pm["k_f"])
    v_rows_p = rows5(pt["v_f"][:N_META], pm["v_f"])
    k_rows_s = pt["k_f"][s_lo:s_hi].reshape(1, n_dec, 1, N_HEADS, HEAD_W)
    v_rows_s = pt["v_f"][s_lo:s_hi].reshape(1, n_dec, 1, N_HEADS, HEAD_W)
    c_p = c_fin[None, None, :, :, :HEAD_W]
    n_p = c_fin[None, None, :, :, HEAD_W]
    m_p = m_fin[None, :, :N_HEADS]
    c_s = c_smp[None]
    n_s = n_smp[None]
    m_s = m_smp[None, :, 0, :N_HEADS]
    conv_p = u_last[None, None, -2:, :]
    conv_s = jnp.stack([state_conv[0][:, 1].astype(F32), u_tail[s_lo:s_hi]], axis=1)[None]
    return (y_prompt, y_sample, k_rows_p, v_rows_p, k_rows_s, v_rows_s,
            c_p, n_p, m_p, c_s, n_s, m_s, conv_p, conv_s)
```

```python
import functools

import jax
import jax.numpy as jnp
from jax import lax
from jax.experimental import pallas as pl
from jax.experimental.pallas import tpu as pltpu

F32 = jnp.float32
BF16 = jnp.bfloat16

N_META = 16
N_HEADS = 8
HEAD_W = 128
DA_QK = 64
PAGE_SIZE = 128
CONV_W = 3
EPS = 1e-6
GROUP_W = N_HEADS * HEAD_W
LAM_INIT = 0.8 - 0.6 * 1.0
TAIL_ROWS = 128
NEG_INF = float("-inf")
LOG2E = 1.4426950408889634
FAR = 1 << 24

VMEM_LIMIT = 52 * 1024 * 1024

_NT = (((1,), (1,)), ((), ()))


def _cparams(n_axes):
    return pltpu.CompilerParams(dimension_semantics=("arbitrary",) * n_axes,
                                vmem_limit_bytes=VMEM_LIMIT)


def _pick(total, pref):
    t = min(pref, total)
    while total % t:
        t //= 2
    return t


def _pre_kernel(x_ref, g_ref, wg_ref, bias_ref, h_ref, ig_ref, lf_ref):
    x = x_ref[...]
    y = x * lax.rsqrt(jnp.mean(x * x, axis=-1, keepdims=True) + EPS) * g_ref[...]
    hb = y.astype(BF16)
    h_ref[...] = hb
    z = jnp.dot(hb, wg_ref[...], preferred_element_type=F32) + bias_ref[...]
    ig_ref[...] = z[:, :HEAD_W]
    zf = z[:, HEAD_W:]
    lf_ref[...] = jnp.minimum(zf, 0.0) - jnp.log1p(jnp.exp(-jnp.abs(zf)))


def _pre(x, g, wg, bias):
    m, d = x.shape
    tm = _pick(m, 512)
    return pl.pallas_call(
        _pre_kernel,
        grid=(m // tm,),
        in_specs=[pl.BlockSpec((tm, d), lambda i: (i, 0)),
                  pl.BlockSpec((1, d), lambda i: (0, 0)),
                  pl.BlockSpec((d, 2 * HEAD_W), lambda i: (0, 0)),
                  pl.BlockSpec((1, 2 * HEAD_W), lambda i: (0, 0))],
        out_specs=[pl.BlockSpec((tm, d), lambda i: (i, 0)),
                   pl.BlockSpec((tm, HEAD_W), lambda i: (i, 0)),
                   pl.BlockSpec((tm, HEAD_W), lambda i: (i, 0))],
        out_shape=[jax.ShapeDtypeStruct((m, d), BF16),
                   jax.ShapeDtypeStruct((m, HEAD_W), F32),
                   jax.ShapeDtypeStruct((m, HEAD_W), F32)],
        compiler_params=_cparams(1),
        name="pre_norm_gates",
    )(x, g, wg, bias)


def _proj_kernel(h_ref, w_ref, s_ref, *rest, with_f32):
    if with_f32:
        of_ref, ob_ref, wb_ref = rest
    else:
        ob_ref, wb_ref = rest

    @pl.when(pl.program_id(1) == 0)
    def _():
        wb_ref[...] = w_ref[...].astype(BF16)

    z = jnp.dot(h_ref[...], wb_ref[...], preferred_element_type=F32)
    if with_f32:
        of_ref[0] = z
    ob_ref[0] = (z * s_ref[0, :, 0:1]).astype(BF16)


def _proj(h, w_in, groups, scales, with_f32):
    m, d = h.shape
    tm = _pick(m, 512)
    ng = len(groups)
    first, rest_off = groups[0], groups[1] - 1 if ng > 1 else 0
    assert list(groups) == [first] + [rest_off + k for k in range(1, ng)]
    s_arr = jnp.broadcast_to(jnp.asarray(scales, F32)[:, None, None], (ng, 1, HEAD_W))

    def w_map(j, i):
        return (0, jnp.where(j == 0, first, j + rest_off))

    out_specs = [pl.BlockSpec((1, tm, GROUP_W), lambda j, i: (j, i, 0))]
    out_shape = [jax.ShapeDtypeStruct((ng, m, GROUP_W), BF16)]
    if with_f32:
        out_specs = [pl.BlockSpec((1, tm, GROUP_W), lambda j, i: (j, i, 0))] + out_specs
        out_shape = [jax.ShapeDtypeStruct((ng, m, GROUP_W), F32)] + out_shape
    outs = pl.pallas_call(
        functools.partial(_proj_kernel, with_f32=with_f32),
        grid=(ng, m // tm),
        in_specs=[pl.BlockSpec((tm, d), lambda j, i: (i, 0)),
                  pl.BlockSpec((d, GROUP_W), w_map),
                  pl.BlockSpec((1, 1, HEAD_W), lambda j, i: (j, 0, 0))],
        out_specs=out_specs,
        out_shape=out_shape,
        scratch_shapes=[pltpu.VMEM((d, GROUP_W), BF16)],
        compiler_params=_cparams(2),
        name="in_proj_f32" if with_f32 else "in_proj",
    )(h, w_in, s_arr)
    return outs if with_f32 else outs[0]


def _attn_kernel(qi_ref, kj_ref, q_ref, k_ref, v_ref, kp_ref, vp_ref, lam_ref, g_ref,
                 o_ref, m_sc, l_sc, acc_sc, *, tq, tk, sub, ahead, q_pos0, prefix_off, n_steps):
    step = pl.program_id(1)
    i = qi_ref[step]
    j = kj_ref[step]
    q = q_ref[...]
    lane = lax.broadcasted_iota(jnp.int32, q.shape, 1)
    zero = jnp.zeros_like(q)
    q2 = jnp.concatenate([jnp.where(lane < DA_QK, q, zero), jnp.where(lane >= DA_QK, q, zero)], axis=0)
    n_sub = 2 * tq // sub

    @pl.when(j == 0)
    def _():
        sp = lax.dot_general(q2, kp_ref[...], _NT, preferred_element_type=F32)
        col = lax.broadcasted_iota(jnp.int32, sp.shape, 1)
        r = lax.broadcasted_iota(jnp.int32, sp.shape, 0)
        ok = (col < N_META) & (col <= jnp.where(r >= tq, r - tq, r) + prefix_off)
        sp = jnp.where(ok, sp, NEG_INF)
        m0 = jnp.max(sp, axis=1, keepdims=True)
        p = jnp.exp2(sp - m0)
        m_sc[...] = jnp.broadcast_to(m0, m_sc.shape)
        l_sc[...] = jnp.broadcast_to(jnp.sum(p, axis=1, keepdims=True), l_sc.shape)
        acc_sc[...] = jnp.dot(p.astype(BF16), vp_ref[...], preferred_element_type=F32)

    def block(masked):
        k = k_ref[...]
        v = v_ref[...]
        if masked:
            rel = (lax.broadcasted_iota(jnp.int32, (sub, tk), 1)
                   - lax.broadcasted_iota(jnp.int32, (sub, tk), 0))
        m_all, l_all, acc_all = m_sc[...], l_sc[...], acc_sc[...]
        new = []

        def scores(r):
            return lax.dot_general(q2[r * sub:(r + 1) * sub], k, _NT, preferred_element_type=F32)

        pending = [scores(r) for r in range(min(ahead, n_sub))]
        for r in range(n_sub):
            rows = slice(r * sub, (r + 1) * sub)
            s = pending.pop(0)
            if r + ahead < n_sub:
                pending.append(scores(r + ahead))
            if masked:
                s = jnp.where(rel <= q_pos0 + i * tq + (r * sub) % tq - j * tk, s, NEG_INF)
            m_prev = m_all[rows]
            m_new = jnp.maximum(m_prev, jnp.max(s, axis=1, keepdims=True))
            alpha = jnp.exp2(m_prev - m_new)
            p = jnp.exp2(s - jnp.concatenate([m_new] * (tk // HEAD_W), axis=1))
            new.append((m_new, alpha * l_all[rows] + jnp.sum(p, axis=1, keepdims=True),
                        alpha * acc_all[rows] + jnp.dot(p.astype(BF16), v, preferred_element_type=F32)))
        for r, (m_new, l_new, acc_new) in enumerate(new):
            rows = slice(r * sub, (r + 1) * sub)
            m_sc[rows] = m_new
            l_sc[rows] = l_new
            acc_sc[rows] = acc_new

    needs_mask = j * tk + (tk - 1) > q_pos0 + i * tq
    pl.when(needs_mask)(lambda: block(True))
    pl.when(jnp.logical_not(needs_mask))(lambda: block(False))

    is_last = jnp.logical_or(step == n_steps - 1, qi_ref[step + 1] != i)

    @pl.when(is_last)
    def _():
        o = acc_sc[...] / l_sc[...]
        lam = (jnp.exp(jnp.sum(lam_ref[0:1, :] * lam_ref[1:2, :], axis=1, keepdims=True))
               - jnp.exp(jnp.sum(lam_ref[2:3, :] * lam_ref[3:4, :], axis=1, keepdims=True)) + LAM_INIT)
        d = o[:tq] - lam * o[tq:]
        y = d * lax.rsqrt(jnp.mean(d * d, axis=-1, keepdims=True) + EPS) * g_ref[...]
        o_ref[...] = (y * (1.0 - LAM_INIT)).astype(o_ref.dtype)


def _attn(q, k, v, kp, vp, lam4, g_da, *, n_q_rows, tq, tk, q_pos0, prefix_off, pairs):
    qi = jnp.asarray([p[0] for p in pairs] + [-1], jnp.int32)
    kj = jnp.asarray([p[1] for p in pairs] + [0], jnp.int32)
    n_steps = len(pairs)
    kern = functools.partial(_attn_kernel, tq=tq, tk=tk, sub=min(256, 2 * tq), ahead=2, q_pos0=q_pos0,
                             prefix_off=prefix_off, n_steps=n_steps)
    grid_spec = pltpu.PrefetchScalarGridSpec(
        num_scalar_prefetch=2,
        grid=(N_HEADS, n_steps),
        in_specs=[pl.BlockSpec((None, tq, HEAD_W), lambda h, s, qi, kj: (q[1], qi[s], h)),
                  pl.BlockSpec((None, tk, HEAD_W), lambda h, s, qi, kj: (k[1], kj[s], h)),
                  pl.BlockSpec((None, tk, HEAD_W), lambda h, s, qi, kj: (v[1], kj[s], h)),
                  pl.BlockSpec((None, TAIL_ROWS, HEAD_W), lambda h, s, qi, kj: (kp[1], 0, h)),
                  pl.BlockSpec((None, TAIL_ROWS, HEAD_W), lambda h, s, qi, kj: (vp[1], 0, h)),
                  pl.BlockSpec((4, DA_QK), lambda h, s, qi, kj: (0, 0)),
                  pl.BlockSpec((1, HEAD_W), lambda h, s, qi, kj: (0, 0))],
        out_specs=pl.BlockSpec((tq, HEAD_W), lambda h, s, qi, kj: (qi[s], h)),
        scratch_shapes=[pltpu.VMEM((2 * tq, HEAD_W), F32), pltpu.VMEM((2 * tq, HEAD_W), F32),
                        pltpu.VMEM((2 * tq, HEAD_W), F32)])
    return pl.pallas_call(
        kern, grid_spec=grid_spec,
        out_shape=jax.ShapeDtypeStruct((n_q_rows, GROUP_W), BF16),
        compiler_params=_cparams(2),
        name="diff_attn_prompt",
    )(qi, kj, q[0], k[0], v[0], kp[0], vp[0], lam4, g_da)


def _mlstm_kernel(q_ref, k_ref, v_ref, o_ref, ig_ref, lf_ref, c0_ref, m0_ref, g_ref,
                  y_ref, c_out_ref, m_out_ref, c_sc, m_sc, *, chunk, n_chunks):
    c = pl.program_id(0)

    @pl.when(c == 0)
    def _():
        c_sc[...] = c0_ref[...]
        m_sc[...] = m0_ref[...]

    L = chunk
    ig = ig_ref[...]
    lf = lf_ref[...]
    row = lax.broadcasted_iota(jnp.int32, (L, L), 0)
    col = lax.broadcasted_iota(jnp.int32, (L, L), 1)
    causal = col <= row
    b_col = jnp.dot(causal.astype(F32), lf, preferred_element_type=F32,
                    precision=lax.Precision.HIGHEST)
    b_row = b_col.T
    ig_row = ig.T
    m_prev_all = m_sc[...]
    lane1 = lax.broadcasted_iota(jnp.int32, (1, HEAD_W), 1)
    m_next_all = m_prev_all
    ones = jnp.ones((L, HEAD_W), BF16)

    for h in range(N_HEADS):
        hs = slice(h * HEAD_W, (h + 1) * HEAD_W)
        bc = b_col[:, h:h + 1]
        m_prev = m_prev_all[:, h:h + 1]
        dm = jnp.where(causal, bc - b_row[h:h + 1, :] + ig_row[h:h + 1, :], NEG_INF)
        inter = bc + m_prev
        m_t = jnp.maximum(inter, jnp.max(dm, axis=1, keepdims=True))
        w = jnp.exp(dm - m_t)
        g = jnp.exp(inter - m_t)
        qh = q_ref[:, hs]
        kh = k_ref[:, hs]
        v_aug = jnp.concatenate([v_ref[:, hs], ones], axis=1)
        s = lax.dot_general(qh, kh, _NT, preferred_element_type=F32)
        a = (s * w).astype(BF16)
        c_h = c_sc[h]
        num_aug = (jnp.dot(a, v_aug, preferred_element_type=F32)
                   + g * jnp.dot(qh, c_h.astype(BF16), preferred_element_type=F32))
        num = num_aug[:, :HEAD_W]
        den = num_aug[:, HEAD_W:]
        hh = num / jnp.maximum(jnp.abs(den), jnp.exp(-m_t))
        y = hh * lax.rsqrt(jnp.mean(hh * hh, axis=-1, keepdims=True) + EPS) * g_ref[:, hs]
        y_ref[:, hs] = (y * jax.nn.sigmoid(o_ref[:, hs].astype(F32))).astype(y_ref.dtype)
        b_last = bc[L - 1:L]
        m_new = m_t[L - 1:L]
        ws = jnp.exp(b_last - bc + ig[:, h:h + 1] - m_new)
        decay = jnp.exp(b_last + m_prev - m_new)
        kw_t = (kh.astype(F32) * ws).T.astype(BF16)
        c_sc[h] = decay * c_h + jnp.dot(kw_t, v_aug, preferred_element_type=F32)
        m_next_all = jnp.where(lane1 == h, m_new, m_next_all)

    m_sc[...] = m_next_all

    @pl.when(c == n_chunks - 1)
    def _():
        c_out_ref[...] = c_sc[...]
        m_out_ref[...] = m_sc[...]


def _mlstm(proj, ig, lf, c0, m0, g_ml, *, n_rows, chunk):
    n_chunks = n_rows // chunk
    row_blk = lambda w: pl.BlockSpec((chunk, w), lambda c: (c, 0))
    grp_blk = lambda g: pl.BlockSpec((None, chunk, GROUP_W), lambda c: (g, c, 0))
    whole = lambda shape: pl.BlockSpec(shape, lambda c: (0,) * len(shape))
    return pl.pallas_call(
        functools.partial(_mlstm_kernel, chunk=chunk, n_chunks=n_chunks),
        grid=(n_chunks,),
        in_specs=[grp_blk(1), grp_blk(2), grp_blk(3), grp_blk(4),
                  row_blk(HEAD_W), row_blk(HEAD_W),
                  whole((N_HEADS, HEAD_W, 2 * HEAD_W)), whole((1, HEAD_W)), whole((1, GROUP_W))],
        out_specs=[row_blk(GROUP_W), whole((N_HEADS, HEAD_W, 2 * HEAD_W)), whole((1, HEAD_W))],
        out_shape=[jax.ShapeDtypeStruct((n_rows, GROUP_W), BF16),
                   jax.ShapeDtypeStruct((N_HEADS, HEAD_W, 2 * HEAD_W), F32),
                   jax.ShapeDtypeStruct((1, HEAD_W), F32)],
        scratch_shapes=[pltpu.VMEM((N_HEADS, HEAD_W, 2 * HEAD_W), F32), pltpu.VMEM((1, HEAD_W), F32)],
        compiler_params=_cparams(1),
        name="mlstm_prompt",
    )(proj, proj, proj, proj, ig, lf, c0, m0, g_ml)


def _decode_kernel(pt_ref, q_ref, kn_ref, vn_ref, lam_ref, g_ref, *rest, pages_per_step, n_steps):
    k_refs = rest[:pages_per_step]
    v_refs = rest[pages_per_step:2 * pages_per_step]
    o_ref, m_sc, l_sc, acc_sc = rest[2 * pages_per_step:]
    j = pl.program_id(1)
    n_rows = 2 * N_HEADS
    page_rows = PAGE_SIZE * N_HEADS
    q16 = q_ref[0]
    rr = lax.broadcasted_iota(jnp.int32, (n_rows, HEAD_W), 0)
    cc = lax.broadcasted_iota(jnp.int32, (n_rows, HEAD_W), 1)
    qm = jnp.where(cc // DA_QK == rr // N_HEADS, q16, 0.0)

    @pl.when(j == 0)
    def _():
        m_sc[...] = jnp.broadcast_to(jnp.sum(qm * kn_ref[0], axis=1, keepdims=True), m_sc.shape)
        l_sc[...] = jnp.ones_like(l_sc)
        acc_sc[...] = vn_ref[0]

    qb = qm.astype(BF16)
    key_head = lax.broadcasted_iota(jnp.int32, (n_rows, page_rows), 1) % N_HEADS
    row_head = lax.broadcasted_iota(jnp.int32, (n_rows, page_rows), 0) % N_HEADS
    own = key_head == row_head
    s_pages = [jnp.where(own, lax.dot_general(qb, kr[0].astype(BF16), _NT, preferred_element_type=F32),
                         NEG_INF) for kr in k_refs]
    s_max = s_pages[0]
    for s in s_pages[1:]:
        s_max = jnp.maximum(s_max, s)
    m_prev = m_sc[...]
    m_new = jnp.maximum(m_prev, jnp.max(s_max, axis=1, keepdims=True))
    alpha = jnp.exp2(m_prev - m_new)
    m_rep = jnp.concatenate([m_new] * (page_rows // HEAD_W), axis=1)
    l_add = jnp.zeros((n_rows, 1), F32)
    pv = jnp.zeros((n_rows, HEAD_W), F32)
    for s, vr in zip(s_pages, v_refs):
        p = jnp.exp2(s - m_rep)
        l_add += jnp.sum(p, axis=1, keepdims=True)
        pv += jnp.dot(p.astype(BF16), vr[0].astype(BF16), preferred_element_type=F32)
    l_sc[...] = alpha * l_sc[...] + l_add
    acc_sc[...] = alpha * acc_sc[...] + pv
    m_sc[...] = m_new

    @pl.when(j == n_steps - 1)
    def _():
        o = acc_sc[...] / l_sc[...]
        lam = (jnp.exp(jnp.sum(lam_ref[0:1, :] * lam_ref[1:2, :], axis=1, keepdims=True))
               - jnp.exp(jnp.sum(lam_ref[2:3, :] * lam_ref[3:4, :], axis=1, keepdims=True)) + LAM_INIT)
        d = o[:N_HEADS] - lam * o[N_HEADS:]
        y = d * lax.rsqrt(jnp.mean(d * d, axis=-1, keepdims=True) + EPS) * g_ref[...]
        o_ref[0] = y * (1.0 - LAM_INIT)


def _decode_attn(page_table, q, k_new, v_new, cache_k, cache_v, lam4, g_da):
    b, n_pages = page_table.shape
    pps = _pick(n_pages, 8)
    n_steps = n_pages // pps
    pt_flat = page_table.reshape(-1)
    n_rows = 2 * N_HEADS

    def page_spec(g):
        return pl.BlockSpec((1, PAGE_SIZE * N_HEADS, HEAD_W),
                            lambda bi, j, pt: (pt[bi * n_pages + j * pps + g], 0, 0))

    seq_spec = pl.BlockSpec((1, n_rows, HEAD_W), lambda bi, j, pt: (bi, 0, 0))
    grid_spec = pltpu.PrefetchScalarGridSpec(
        num_scalar_prefetch=1,
        grid=(b, n_steps),
        in_specs=[seq_spec, seq_spec, seq_spec,
                  pl.BlockSpec((4, DA_QK), lambda bi, j, pt: (0, 0)),
                  pl.BlockSpec((1, HEAD_W), lambda bi, j, pt: (0, 0))]
                 + [page_spec(g) for g in range(pps)] + [page_spec(g) for g in range(pps)],
        out_specs=pl.BlockSpec((1, N_HEADS, HEAD_W), lambda bi, j, pt: (bi, 0, 0)),
        scratch_shapes=[pltpu.VMEM((n_rows, HEAD_W), F32), pltpu.VMEM((n_rows, HEAD_W), F32),
                        pltpu.VMEM((n_rows, HEAD_W), F32)])
    return pl.pallas_call(
        functools.partial(_decode_kernel, pages_per_step=pps, n_steps=n_steps),
        grid_spec=grid_spec,
        out_shape=jax.ShapeDtypeStruct((b, N_HEADS, HEAD_W), F32),
        compiler_params=_cparams(2),
        name="diff_attn_decode",
    )(pt_flat, q, k_new, v_new, lam4, g_da, *([cache_k] * pps), *([cache_v] * pps))


def _mlstm_step_kernel(q_ref, k_ref, v_ref, o_ref, ig_ref, lf_ref, c_ref, n_ref, m_ref, g_ref,
                       y_ref, c_out_ref, n_out_ref, m_out_ref):
    q8 = q_ref[0]
    k8 = k_ref[0]
    v8 = v_ref[0]
    n8 = n_ref[0]
    ig = ig_ref[0]
    lf = lf_ref[0]
    m = m_ref[0]
    inter = lf + m
    m_t = jnp.maximum(inter, ig)
    w_all = jnp.exp(ig - m_t)
    g_all = jnp.exp(inter - m_t)
    floor_all = jnp.exp(-m_t)
    m_out_ref[0] = m_t
    q_t = q8.T
    k_t = k8.T
    qk = jnp.sum(q8 * k8, axis=1, keepdims=True)
    qn = jnp.sum(q8 * n8, axis=1, keepdims=True)
    for h in range(N_HEADS):
        w = w_all[:, h:h + 1]
        g = g_all[:, h:h + 1]
        c_h = c_ref[0, h]
        vh = v8[h:h + 1]
        a = qk[h:h + 1] * w
        q_c = jnp.sum(q_t[:, h:h + 1] * c_h, axis=0, keepdims=True)
        num = a * vh + g * q_c
        den = a + g * qn[h:h + 1]
        hh = num / jnp.maximum(jnp.abs(den), floor_all[:, h:h + 1])
        y = hh * lax.rsqrt(jnp.mean(hh * hh, axis=-1, keepdims=True) + EPS) * g_ref[h:h + 1]
        y_ref[0, h:h + 1] = (y * jax.nn.sigmoid(o_ref[0, h:h + 1].astype(F32))).astype(y_ref.dtype)
        c_out_ref[0, h] = g * c_h + w * (k_t[:, h:h + 1] * vh)
        n_out_ref[0, h:h + 1] = g * n8[h:h + 1] + w * k8[h:h + 1]


def _mlstm_step(q, k, v, o, ig, lf, c, n, m, g_ml):
    b = q.shape[0]
    hv = pl.BlockSpec((1, N_HEADS, HEAD_W), lambda i: (i, 0, 0))
    gate = pl.BlockSpec((1, 1, HEAD_W), lambda i: (i, 0, 0))
    cs = pl.BlockSpec((1, N_HEADS, HEAD_W, HEAD_W), lambda i: (i, 0, 0, 0))
    return pl.pallas_call(
        _mlstm_step_kernel,
        grid=(b,),
        in_specs=[hv, hv, hv, hv, gate, gate, cs, hv, gate,
                  pl.BlockSpec((N_HEADS, HEAD_W), lambda i: (0, 0))],
        out_specs=[hv, cs, hv, gate],
        out_shape=[jax.ShapeDtypeStruct((b, N_HEADS, HEAD_W), F32),
                   jax.ShapeDtypeStruct(c.shape, F32),
                   jax.ShapeDtypeStruct((b, N_HEADS, HEAD_W), F32),
                   jax.ShapeDtypeStruct((b, 1, HEAD_W), F32)],
        compiler_params=_cparams(1),
        name="mlstm_sample",
    )(q, k, v, o, ig, lf, c, n, m, g_ml)


def _out_proj_kernel(da_ref, ml_ref, x_ref, w_ref, o_ref, wb_ref):
    @pl.when(pl.program_id(1) == 0)
    def _():
        wb_ref[...] = w_ref[...].astype(BF16)

    y = jnp.concatenate([da_ref[...], ml_ref[...]], axis=1)
    o_ref[...] = x_ref[...] + jnp.dot(y, wb_ref[...], preferred_element_type=F32)


def _out_proj(da, ml, x, w_out):
    m, d = x.shape
    tm = _pick(m, 512)
    tn = _pick(d, 1024)
    return pl.pallas_call(
        _out_proj_kernel,
        grid=(d // tn, m // tm),
        in_specs=[pl.BlockSpec((tm, GROUP_W), lambda j, i: (i, 0)),
                  pl.BlockSpec((tm, GROUP_W), lambda j, i: (i, 0)),
                  pl.BlockSpec((tm, tn), lambda j, i: (i, j)),
                  pl.BlockSpec((2 * GROUP_W, tn), lambda j, i: (0, j))],
        out_specs=pl.BlockSpec((tm, tn), lambda j, i: (i, j)),
        out_shape=jax.ShapeDtypeStruct((m, d), F32),
        scratch_shapes=[pltpu.VMEM((2 * GROUP_W, tn), BF16)],
        compiler_params=_cparams(2),
        name="out_proj",
    )(da, ml, x, w_out)


def _ffn_kernel(*refs, sequential, tm, tf, n_f):
    if sequential:
        (x_ref, gf_ref, gl_ref, wa_ref, wg_ref, wd_ref, cwa_ref, cwg_ref, cba_ref, cbg_ref,
         pa_ref, pg_ref, y_ref, ua_ref, ug_ref, h_sc, acc_sc, carry_sc, u_sc) = refs
    else:
        (x_ref, gf_ref, gl_ref, wa_ref, wg_ref, wd_ref, cwa_ref, cwg_ref, cba_ref, cbg_ref,
         s0a_ref, s1a_ref, s0g_ref, s1g_ref, y_ref, ua_ref, ug_ref, h_sc, acc_sc) = refs
    t = pl.program_id(0)
    f = pl.program_id(1)
    w_sub = min(tf, 256)
    n_c = tf // w_sub

    @pl.when(f == 0)
    def _():
        x = x_ref[...]
        h_sc[...] = (x * lax.rsqrt(jnp.mean(x * x, axis=-1, keepdims=True) + EPS)
                     * gf_ref[...]).astype(BF16)
        acc_sc[...] = jnp.zeros_like(acc_sc)

    h = h_sc[...]

    if sequential:
        for which, p_ref in enumerate((pa_ref, pg_ref)):
            @pl.when(t == 0)
            def _():
                carry_sc[which, f] = p_ref[...]

            u_sc[which, 0:8, :] = carry_sc[which, f]

    def up(c):
        cs = slice(c * w_sub, (c + 1) * w_sub)
        return (jnp.dot(h, wa_ref[:, cs], preferred_element_type=F32),
                jnp.dot(h, wg_ref[:, cs], preferred_element_type=F32))

    def conv(u, which, c, cw_ref, cb_ref, u_out_ref):
        cs = slice(c * w_sub, (c + 1) * w_sub)
        if sequential:
            u_sc[which, 8:8 + tm, cs] = u
            s1 = u_sc[which, 7:7 + tm, cs]
            s0 = u_sc[which, 6:6 + tm, cs]
        else:
            s0 = (s0a_ref, s0g_ref)[which][:, cs]
            s1 = (s1a_ref, s1g_ref)[which][:, cs]
            u_out_ref[:, cs] = u
        return cb_ref[:, cs] + cw_ref[0:1, cs] * s0 + cw_ref[1:2, cs] * s1 + cw_ref[2:3, cs] * u

    pending = [up(0)]
    down = None
    for c in range(n_c):
        u_a, u_g = pending.pop(0)
        if c + 1 < n_c:
            pending.append(up(c + 1))
        ya = conv(u_a, 0, c, cwa_ref, cba_ref, ua_ref)
        yg = conv(u_g, 1, c, cwg_ref, cbg_ref, ug_ref)
        act = (ya * jax.nn.sigmoid(ya) * yg).astype(BF16)
        part = jnp.dot(act, wd_ref[c * w_sub:(c + 1) * w_sub, :], preferred_element_type=F32)
        down = part if down is None else down + part
    acc_sc[...] += down

    if sequential:
        for which, u_out_ref in enumerate((ua_ref, ug_ref)):
            tail8 = u_sc[which, tm:tm + 8, :]
            carry_sc[which, f] = tail8
            u_out_ref[...] = tail8

    @pl.when(f == n_f - 1)
    def _():
        x2 = x_ref[...] + acc_sc[...]
        y_ref[...] = x2 * lax.rsqrt(jnp.mean(x2 * x2, axis=-1, keepdims=True) + EPS) * gl_ref[...]


def _ffn(x, g_ffn, g_final, w_up, w_down, conv_w, conv_b, *, prefix=None, s0=None, s1=None):
    m, d = x.shape
    d_ff = w_down.shape[0]
    sequential = prefix is not None
    tm = _pick(m, 512)
    tf = _pick(d_ff, 512)
    n_f = d_ff // tf
    n_t = m // tm
    col_a = lambda r, w: pl.BlockSpec((r, w), lambda t, f: (0, f))
    col_g = lambda r, w: pl.BlockSpec((r, w), lambda t, f: (0, f + n_f))
    in_specs = [pl.BlockSpec((tm, d), lambda t, f: (t, 0)),
                pl.BlockSpec((1, d), lambda t, f: (0, 0)),
                pl.BlockSpec((1, d), lambda t, f: (0, 0)),
                col_a(d, tf), col_g(d, tf),
                pl.BlockSpec((tf, d), lambda t, f: (f, 0)),
                col_a(CONV_W, tf), col_g(CONV_W, tf), col_a(1, tf), col_g(1, tf)]
    args = [x, g_ffn, g_final, w_up, w_up, w_down, conv_w, conv_w, conv_b, conv_b]
    scratch = [pltpu.VMEM((tm, d), BF16), pltpu.VMEM((tm, d), F32)]
    if sequential:
        in_specs += [col_a(8, tf), col_g(8, tf)]
        args += [prefix, prefix]
        u_rows = 8 * n_t
        u_spec = pl.BlockSpec((8, tf), lambda t, f: (t, f))
        scratch += [pltpu.VMEM((2, n_f, 8, tf), F32), pltpu.VMEM((2, tm + 8, tf), F32)]
    else:
        rows_a = pl.BlockSpec((tm, tf), lambda t, f: (t, f))
        rows_g = pl.BlockSpec((tm, tf), lambda t, f: (t, f + n_f))
        in_specs += [rows_a, rows_a, rows_g, rows_g]
        args += [s0, s1, s0, s1]
        u_rows = m
        u_spec = rows_a
    y, ua, ug = pl.pallas_call(
        functools.partial(_ffn_kernel, sequential=sequential, tm=tm, tf=tf, n_f=n_f),
        grid=(n_t, n_f),
        in_specs=in_specs,
        out_specs=[pl.BlockSpec((tm, d), lambda t, f: (t, 0)), u_spec, u_spec],
        out_shape=[jax.ShapeDtypeStruct((m, d), F32),
                   jax.ShapeDtypeStruct((u_rows, d_ff), F32),
                   jax.ShapeDtypeStruct((u_rows, d_ff), F32)],
        scratch_shapes=scratch,
        compiler_params=_cparams(2),
        name="conv_ffn_seq" if sequential else "conv_ffn_rows",
    )(*args)
    return y, jnp.concatenate([ua, ug], axis=1)


def kernel(x_prompt, x_sample, cache_k, cache_v, page_table, state_C, state_n, state_m, state_conv,
           meta_tokens, g_mix, w_in, b_i, b_f, lam_q1, lam_k1, lam_q2, lam_k2, g_da, g_ml, w_out,
           g_ffn, w_up, conv_w, conv_b, w_down, g_final):
    depth = w_in.shape[0]
    assert depth == 1, "single-layer trunk"
    assert x_prompt.shape[0] == 1 and x_sample.shape[1] == 1
    seq, d = x_prompt.shape[1:]
    n_dec = x_sample.shape[0]
    assert N_META + n_dec <= TAIL_ROWS
    d_ff = w_down.shape[1]
    n_pool = cache_k.shape[1]
    gate_off = 7 * GROUP_W
    s_lo, s_hi = N_META, N_META + n_dec

    x_main = x_prompt[0]
    x_tail = jnp.concatenate([meta_tokens.astype(F32), x_sample[:, 0],
                              jnp.zeros((TAIL_ROWS - s_hi, d), F32)], axis=0)
    w_in0 = w_in[0]
    wg = jnp.zeros((d, 2 * HEAD_W), F32)
    wg = wg.at[:, :N_HEADS].set(w_in0[:, gate_off:gate_off + N_HEADS])
    wg = wg.at[:, HEAD_W:HEAD_W + N_HEADS].set(w_in0[:, gate_off + N_HEADS:gate_off + 2 * N_HEADS])
    wg = wg.astype(BF16)
    gbias = jnp.zeros((1, 2 * HEAD_W), F32)
    gbias = gbias.at[0, :N_HEADS].set(b_i[0]).at[0, HEAD_W:HEAD_W + N_HEADS].set(b_f[0])
    lam4 = jnp.stack([lam_q1[0], lam_k1[0], lam_q2[0], lam_k2[0]]).astype(F32)
    g_da2 = g_da[0].reshape(1, HEAD_W)
    g_ml2 = g_ml[0].reshape(1, GROUP_W)
    g_mix2 = g_mix[0].reshape(1, d)
    g_ffn2 = g_ffn[0].reshape(1, d)
    g_fin2 = g_final.reshape(1, d)
    w_up_b = w_up[0].astype(BF16)
    w_down_b = w_down[0].astype(BF16)
    conv_w2 = conv_w[0]
    conv_b2 = conv_b[0].reshape(1, 2 * d_ff)
    k_scale = HEAD_W ** -0.5
    q_scale = DA_QK ** -0.5 * LOG2E

    def project(x):
        h, ig, lf = _pre(x, g_mix2, wg, gbias)
        kv_f, kv_b = _proj(h, w_in0, (1, 2), (1.0, 1.0), True)
        oth = _proj(h, w_in0, (0, 3, 4, 5, 6), (q_scale, 1.0, k_scale, 1.0, 1.0), False)
        return dict(kv_f=kv_f, kv_b=kv_b, oth=oth, ig=ig, lf=lf)

    pm = project(x_main)
    pt = project(x_tail)
    pt.update(k_f=pt["kv_f"][0], v_f=pt["kv_f"][1], dq=pt["oth"][0], mq=pt["oth"][1],
              mk=pt["oth"][2], mv=pt["oth"][3], mo=pt["oth"][4])

    tk = _pick(seq, 512)
    tq = _pick(seq, 1024)
    pairs = [(i, j) for i in range(seq // tq) for j in range((i + 1) * tq // tk)]
    kv_main = ((pm["kv_b"], 0), (pm["kv_b"], 1))
    kv_meta = ((pt["kv_b"], 0), (pt["kv_b"], 1))
    da_main = _attn((pm["oth"], 0), *kv_main, *kv_meta, lam4, g_da2,
                    n_q_rows=seq, tq=tq, tk=tk, q_pos0=0, prefix_off=FAR, pairs=pairs)
    da_meta = _attn((pt["oth"], 0), *kv_main, *kv_meta, lam4, g_da2,
                    n_q_rows=N_META, tq=N_META, tk=tk, q_pos0=-FAR, prefix_off=0, pairs=[(0, 0)])

    c_zero = jnp.zeros((N_HEADS, HEAD_W, 2 * HEAD_W), F32)
    m_zero = jnp.zeros((1, HEAD_W), F32)
    ml_meta, c_meta, m_meta = _mlstm(pt["oth"], pt["ig"], pt["lf"], c_zero, m_zero, g_ml2,
                                     n_rows=N_META, chunk=N_META)
    ml_main, c_fin, m_fin = _mlstm(pm["oth"], pm["ig"], pm["lf"], c_meta, m_meta, g_ml2,
                                   n_rows=seq, chunk=_pick(seq, 256))

    f32_rows = lambda a: a[s_lo:s_hi].astype(F32)
    heads = lambda a: a.reshape(n_dec, N_HEADS, HEAD_W)
    twice = lambda a: jnp.concatenate([heads(a), heads(a)], axis=1)
    ck = cache_k[0].reshape(n_pool, PAGE_SIZE * N_HEADS, HEAD_W)
    cv = cache_v[0].reshape(n_pool, PAGE_SIZE * N_HEADS, HEAD_W)
    da_smp = _decode_attn(page_table, twice(f32_rows(pt["dq"])), twice(pt["k_f"][s_lo:s_hi]),
                          twice(pt["v_f"][s_lo:s_hi]), ck, cv, lam4, g_da2)
    da_smp = da_smp.reshape(n_dec, GROUP_W).astype(BF16)
    m_pad = jnp.zeros((n_dec, 1, HEAD_W), F32).at[:, 0, :N_HEADS].set(state_m[0].astype(F32))
    ml_smp, c_smp, n_smp, m_smp = _mlstm_step(
        heads(f32_rows(pt["mq"])), heads(f32_rows(pt["mk"])), heads(f32_rows(pt["mv"])),
        heads(pt["mo"][s_lo:s_hi]), pt["ig"][s_lo:s_hi, None, :], pt["lf"][s_lo:s_hi, None, :],
        state_C[0].astype(F32), state_n[0].astype(F32), m_pad, g_ml2.reshape(N_HEADS, HEAD_W))
    ml_smp = ml_smp.reshape(n_dec, GROUP_W).astype(BF16)

    pad_rows = jnp.zeros((TAIL_ROWS - s_hi, GROUP_W), BF16)
    da_tail = jnp.concatenate([da_meta, da_smp, pad_rows], axis=0)
    ml_tail = jnp.concatenate([ml_meta, ml_smp, pad_rows], axis=0)
    x1_main = _out_proj(da_main, ml_main, x_main, w_out[0])
    x1_tail = _out_proj(da_tail, ml_tail, x_tail, w_out[0])

    hist = jnp.zeros((2, TAIL_ROWS, 2 * d_ff), F32).at[:, s_lo:s_hi].set(
        jnp.swapaxes(state_conv[0].astype(F32), 0, 1))
    y_tail, u_tail = _ffn(x1_tail, g_ffn2, g_fin2, w_up_b, w_down_b, conv_w2, conv_b2,
                          s0=hist[0], s1=hist[1])
    y_main, u_last = _ffn(x1_main, g_ffn2, g_fin2, w_up_b, w_down_b, conv_w2, conv_b2,
                          prefix=u_tail[N_META - 8:N_META])

    def rows5(meta_part, main_part):
        return jnp.concatenate([meta_part, main_part], axis=0).reshape(
            1, 1, N_META + seq, N_HEADS, HEAD_W)

    y_prompt = y_main[None]
    y_sample = y_tail[s_lo:s_hi, None, :]
    k_rows_p = rows5(pt["k_f"][:N_META], pm["kv_f"][0])
    v_rows_p = rows5(pt["v_f"][:N_META], pm["kv_f"][1])
    k_rows_s = pt["k_f"][s_lo:s_hi].reshape(1, n_dec, 1, N_HEADS, HEAD_W)
    v_rows_s = pt["v_f"][s_lo:s_hi].reshape(1, n_dec, 1, N_HEADS, HEAD_W)
    c_p = c_fin[None, None, :, :, :HEAD_W]
    n_p = c_fin[None, None, :, :, HEAD_W]
    m_p = m_fin[None, :, :N_HEADS]
    c_s = c_smp[None]
    n_s = n_smp[None]
    m_s = m_smp[None, :, 0, :N_HEADS]
    conv_p = u_last[None, None, -2:, :]
    conv_s = jnp.stack([state_conv[0][:, 1].astype(F32), u_tail[s_lo:s_hi]], axis=1)[None]
    return (y_prompt, y_sample, k_rows_p, v_rows_p, k_rows_s, v_rows_s,
            c_p, n_p, m_p, c_s, n_s, m_s, conv_p, conv_s)
```

```python
import functools

import jax
import jax.numpy as jnp
from jax import lax
from jax.experimental import pallas as pl
from jax.experimental.pallas import tpu as pltpu

F32 = jnp.float32
BF16 = jnp.bfloat16

N_META = 16
N_HEADS = 8
HEAD_W = 128
DA_QK = 64
PAGE_SIZE = 128
CONV_W = 3
EPS = 1e-6
GROUP_W = N_HEADS * HEAD_W
LAM_INIT = 0.8 - 0.6 * 1.0
TAIL_ROWS = 128
NEG_INF = float("-inf")
LOG2E = 1.4426950408889634
FAR = 1 << 24

VMEM_LIMIT = 52 * 1024 * 1024

_NT = (((1,), (1,)), ((), ()))


def _cparams(n_axes):
    return pltpu.CompilerParams(dimension_semantics=("arbitrary",) * n_axes,
                                vmem_limit_bytes=VMEM_LIMIT)


def _pick(total, pref):
    t = min(pref, total)
    while total % t:
        t //= 2
    return t


def _pre_kernel(x_ref, g_ref, wg_ref, bias_ref, h_ref, ig_ref, lf_ref):
    x = x_ref[...]
    y = x * lax.rsqrt(jnp.mean(x * x, axis=-1, keepdims=True) + EPS) * g_ref[...]
    hb = y.astype(BF16)
    h_ref[...] = hb
    z = jnp.dot(hb, wg_ref[...], preferred_element_type=F32) + bias_ref[...]
    ig_ref[...] = z[:, :HEAD_W]
    zf = z[:, HEAD_W:]
    lf_ref[...] = jnp.minimum(zf, 0.0) - jnp.log1p(jnp.exp(-jnp.abs(zf)))


def _pre(x, g, wg, bias):
    m, d = x.shape
    tm = _pick(m, 512)
    return pl.pallas_call(
        _pre_kernel,
        grid=(m // tm,),
        in_specs=[pl.BlockSpec((tm, d), lambda i: (i, 0)),
                  pl.BlockSpec((1, d), lambda i: (0, 0)),
                  pl.BlockSpec((d, 2 * HEAD_W), lambda i: (0, 0)),
                  pl.BlockSpec((1, 2 * HEAD_W), lambda i: (0, 0))],
        out_specs=[pl.BlockSpec((tm, d), lambda i: (i, 0)),
                   pl.BlockSpec((tm, HEAD_W), lambda i: (i, 0)),
                   pl.BlockSpec((tm, HEAD_W), lambda i: (i, 0))],
        out_shape=[jax.ShapeDtypeStruct((m, d), BF16),
                   jax.ShapeDtypeStruct((m, HEAD_W), F32),
                   jax.ShapeDtypeStruct((m, HEAD_W), F32)],
        compiler_params=_cparams(1),
        name="pre_norm_gates",
    )(x, g, wg, bias)


def _proj_kernel(h_ref, w_ref, s_ref, *rest, with_f32):
    if with_f32:
        of_ref, ob_ref, wb_ref = rest
    else:
        ob_ref, wb_ref = rest

    @pl.when(pl.program_id(1) == 0)
    def _():
        wb_ref[...] = w_ref[...].astype(BF16)

    z = jnp.dot(h_ref[...], wb_ref[...], preferred_element_type=F32)
    if with_f32:
        for hd in range(N_HEADS):
            of_ref[0, :, hd, :] = z[:, hd * HEAD_W:(hd + 1) * HEAD_W]
    ob_ref[0] = (z * s_ref[0, :, 0:1]).astype(BF16)


def _proj(h, w_in, groups, scales, with_f32):
    m, d = h.shape
    tm = _pick(m, 1024)
    ng = len(groups)
    first, rest_off = groups[0], groups[1] - 1 if ng > 1 else 0
    assert list(groups) == [first] + [rest_off + k for k in range(1, ng)]
    s_arr = jnp.broadcast_to(jnp.asarray(scales, F32)[:, None, None], (ng, 1, HEAD_W))

    def w_map(j, i):
        return (0, jnp.where(j == 0, first, j + rest_off))

    out_specs = [pl.BlockSpec((1, tm, GROUP_W), lambda j, i: (j, i, 0))]
    out_shape = [jax.ShapeDtypeStruct((ng, m, GROUP_W), BF16)]
    if with_f32:
        out_specs = [pl.BlockSpec((1, tm, N_HEADS, HEAD_W), lambda j, i: (j, i, 0, 0))] + out_specs
        out_shape = [jax.ShapeDtypeStruct((ng, m, N_HEADS, HEAD_W), F32)] + out_shape
    outs = pl.pallas_call(
        functools.partial(_proj_kernel, with_f32=with_f32),
        grid=(ng, m // tm),
        in_specs=[pl.BlockSpec((tm, d), lambda j, i: (i, 0)),
                  pl.BlockSpec((d, GROUP_W), w_map),
                  pl.BlockSpec((1, 1, HEAD_W), lambda j, i: (j, 0, 0))],
        out_specs=out_specs,
        out_shape=out_shape,
        scratch_shapes=[pltpu.VMEM((d, GROUP_W), BF16)],
        compiler_params=_cparams(2),
        name="in_proj_f32" if with_f32 else "in_proj",
    )(h, w_in, s_arr)
    return outs if with_f32 else outs[0]


def _attn_kernel(qi_ref, kj_ref, q_ref, k_ref, v_ref, kp_ref, vp_ref, lam_ref, g_ref,
                 o_ref, m_sc, l_sc, acc_sc, *, tq, tk, sub, ahead, q_pos0, prefix_off, n_steps,
                 mask_deltas):
    step = pl.program_id(1)
    i = qi_ref[step]
    j = kj_ref[step]
    q = q_ref[...]
    lane = lax.broadcasted_iota(jnp.int32, q.shape, 1)
    zero = jnp.zeros_like(q)
    q2 = jnp.concatenate([jnp.where(lane < DA_QK, q, zero), jnp.where(lane >= DA_QK, q, zero)], axis=0)
    n_sub = 2 * tq // sub

    @pl.when(j == 0)
    def _():
        sp = lax.dot_general(q2, kp_ref[...], _NT, preferred_element_type=F32)
        col = lax.broadcasted_iota(jnp.int32, sp.shape, 1)
        r = lax.broadcasted_iota(jnp.int32, sp.shape, 0)
        ok = (col < N_META) & (col <= jnp.where(r >= tq, r - tq, r) + prefix_off)
        sp = jnp.where(ok, sp, NEG_INF)
        m0 = jnp.max(sp, axis=1, keepdims=True)
        p = jnp.exp2(sp - m0)
        m_sc[...] = jnp.broadcast_to(m0, m_sc.shape)
        l_sc[...] = jnp.broadcast_to(jnp.sum(p, axis=1, keepdims=True), l_sc.shape)
        acc_sc[...] = jnp.dot(p.astype(BF16), vp_ref[...], preferred_element_type=F32)

    def block(delta):
        def first_visible(r):
            return FAR if delta is None else delta + (r * sub) % tq
        live = [r for r in range(n_sub) if first_visible(r) + sub - 1 >= 0]
        if not live:
            return
        k = k_ref[...]
        v = v_ref[...]
        rel = (lax.broadcasted_iota(jnp.int32, (sub, tk), 1)
               - lax.broadcasted_iota(jnp.int32, (sub, tk), 0))
        m_all, l_all, acc_all = m_sc[...], l_sc[...], acc_sc[...]
        new = []

        def scores(r):
            return lax.dot_general(q2[r * sub:(r + 1) * sub], k, _NT, preferred_element_type=F32)

        pending = [scores(r) for r in live[:ahead]]
        for n, r in enumerate(live):
            rows = slice(r * sub, (r + 1) * sub)
            s = pending.pop(0)
            if n + ahead < len(live):
                pending.append(scores(live[n + ahead]))
            if first_visible(r) < tk - 1:
                s = jnp.where(rel <= first_visible(r), s, NEG_INF)
            m_prev = m_all[rows]
            m_new = jnp.maximum(m_prev, jnp.max(s, axis=1, keepdims=True))
            alpha = jnp.exp2(m_prev - m_new)
            p = jnp.exp2(s - jnp.concatenate([m_new] * (tk // HEAD_W), axis=1))
            new.append((m_new, alpha * l_all[rows] + jnp.sum(p, axis=1, keepdims=True),
                        alpha * acc_all[rows] + jnp.dot(p.astype(BF16), v, preferred_element_type=F32)))
        for r, (m_new, l_new, acc_new) in zip(live, new):
            rows = slice(r * sub, (r + 1) * sub)
            m_sc[rows] = m_new
            l_sc[rows] = l_new
            acc_sc[rows] = acc_new

    delta_now = q_pos0 + i * tq - j * tk
    pl.when(delta_now >= tk - 1)(lambda: block(None))
    for delta in mask_deltas:
        pl.when(delta_now == delta)(functools.partial(block, delta))

    is_last = jnp.logical_or(step == n_steps - 1, qi_ref[step + 1] != i)

    @pl.when(is_last)
    def _():
        o = acc_sc[...] / l_sc[...]
        lam = (jnp.exp(jnp.sum(lam_ref[0:1, :] * lam_ref[1:2, :], axis=1, keepdims=True))
               - jnp.exp(jnp.sum(lam_ref[2:3, :] * lam_ref[3:4, :], axis=1, keepdims=True)) + LAM_INIT)
        d = o[:tq] - lam * o[tq:]
        y = d * lax.rsqrt(jnp.mean(d * d, axis=-1, keepdims=True) + EPS) * g_ref[...]
        o_ref[...] = (y * (1.0 - LAM_INIT)).astype(o_ref.dtype)


def _attn(q, k, v, kp, vp, lam4, g_da, *, n_q_rows, tq, tk, q_pos0, prefix_off, pairs):
    qi = jnp.asarray([p[0] for p in pairs] + [-1], jnp.int32)
    kj = jnp.asarray([p[1] for p in pairs] + [0], jnp.int32)
    n_steps = len(pairs)
    deltas = {q_pos0 + pi * tq - pj * tk for pi, pj in pairs}
    kern = functools.partial(_attn_kernel, tq=tq, tk=tk, sub=min(256, 2 * tq), ahead=2, q_pos0=q_pos0,
                             prefix_off=prefix_off, n_steps=n_steps,
                             mask_deltas=tuple(sorted(dl for dl in deltas if dl < tk - 1)))
    grid_spec = pltpu.PrefetchScalarGridSpec(
        num_scalar_prefetch=2,
        grid=(N_HEADS, n_steps),
        in_specs=[pl.BlockSpec((None, tq, HEAD_W), lambda h, s, qi, kj: (q[1], qi[s], h)),
                  pl.BlockSpec((None, tk, HEAD_W), lambda h, s, qi, kj: (k[1], kj[s], h)),
                  pl.BlockSpec((None, tk, HEAD_W), lambda h, s, qi, kj: (v[1], kj[s], h)),
                  pl.BlockSpec((None, TAIL_ROWS, HEAD_W), lambda h, s, qi, kj: (kp[1], 0, h)),
                  pl.BlockSpec((None, TAIL_ROWS, HEAD_W), lambda h, s, qi, kj: (vp[1], 0, h)),
                  pl.BlockSpec((4, DA_QK), lambda h, s, qi, kj: (0, 0)),
                  pl.BlockSpec((1, HEAD_W), lambda h, s, qi, kj: (0, 0))],
        out_specs=pl.BlockSpec((tq, HEAD_W), lambda h, s, qi, kj: (qi[s], h)),
        scratch_shapes=[pltpu.VMEM((2 * tq, HEAD_W), F32), pltpu.VMEM((2 * tq, HEAD_W), F32),
                        pltpu.VMEM((2 * tq, HEAD_W), F32)])
    return pl.pallas_call(
        kern, grid_spec=grid_spec,
        out_shape=jax.ShapeDtypeStruct((n_q_rows, GROUP_W), BF16),
        compiler_params=_cparams(2),
        name="diff_attn_prompt",
    )(qi, kj, q[0], k[0], v[0], kp[0], vp[0], lam4, g_da)


def _mlstm_kernel(q_ref, k_ref, v_ref, o_ref, ig_ref, lf_ref, c0_ref, m0_ref, g_ref,
                  y_ref, c_out_ref, m_out_ref, c_sc, m_sc, *, chunk, n_chunks):
    c = pl.program_id(0)

    @pl.when(c == 0)
    def _():
        c_sc[...] = c0_ref[...]
        m_sc[...] = m0_ref[...]

    L = chunk
    ig = ig_ref[...]
    lf = lf_ref[...]
    row = lax.broadcasted_iota(jnp.int32, (L, L), 0)
    col = lax.broadcasted_iota(jnp.int32, (L, L), 1)
    causal = col <= row
    b_col = jnp.dot(causal.astype(F32), lf, preferred_element_type=F32,
                    precision=lax.Precision.HIGHEST)
    b_row = b_col.T
    ig_row = ig.T
    m_prev_all = m_sc[...]
    lane1 = lax.broadcasted_iota(jnp.int32, (1, HEAD_W), 1)
    m_next_all = m_prev_all
    ones = jnp.ones((L, HEAD_W), BF16)

    for h in range(N_HEADS):
        hs = slice(h * HEAD_W, (h + 1) * HEAD_W)
        bc = b_col[:, h:h + 1]
        m_prev = m_prev_all[:, h:h + 1]
        dm = jnp.where(causal, bc - b_row[h:h + 1, :] + ig_row[h:h + 1, :], NEG_INF)
        inter = bc + m_prev
        m_t = jnp.maximum(inter, jnp.max(dm, axis=1, keepdims=True))
        w = jnp.exp(dm - m_t)
        g = jnp.exp(inter - m_t)
        qh = q_ref[:, hs]
        kh = k_ref[:, hs]
        v_aug = jnp.concatenate([v_ref[:, hs], ones], axis=1)
        s = lax.dot_general(qh, kh, _NT, preferred_element_type=F32)
        a = (s * w).astype(BF16)
        c_h = c_sc[h]
        num_aug = (jnp.dot(a, v_aug, preferred_element_type=F32)
                   + g * jnp.dot(qh, c_h.astype(BF16), preferred_element_type=F32))
        num = num_aug[:, :HEAD_W]
        den = num_aug[:, HEAD_W:]
        hh = num / jnp.maximum(jnp.abs(den), jnp.exp(-m_t))
        y = hh * lax.rsqrt(jnp.mean(hh * hh, axis=-1, keepdims=True) + EPS) * g_ref[:, hs]
        y_ref[:, hs] = (y * jax.nn.sigmoid(o_ref[:, hs].astype(F32))).astype(y_ref.dtype)
        b_last = bc[L - 1:L]
        m_new = m_t[L - 1:L]
        ws = jnp.exp(b_last - bc + ig[:, h:h + 1] - m_new)
        decay = jnp.exp(b_last + m_prev - m_new)
        kw_t = (kh.astype(F32) * ws).T.astype(BF16)
        c_sc[h] = decay * c_h + jnp.dot(kw_t, v_aug, preferred_element_type=F32)
        m_next_all = jnp.where(lane1 == h, m_new, m_next_all)

    m_sc[...] = m_next_all

    @pl.when(c == n_chunks - 1)
    def _():
        c_out_ref[...] = c_sc[...]
        m_out_ref[...] = m_sc[...]


def _mlstm(proj, ig, lf, c0, m0, g_ml, *, n_rows, chunk):
    n_chunks = n_rows // chunk
    row_blk = lambda w: pl.BlockSpec((chunk, w), lambda c: (c, 0))
    grp_blk = lambda g: pl.BlockSpec((None, chunk, GROUP_W), lambda c: (g, c, 0))
    whole = lambda shape: pl.BlockSpec(shape, lambda c: (0,) * len(shape))
    return pl.pallas_call(
        functools.partial(_mlstm_kernel, chunk=chunk, n_chunks=n_chunks),
        grid=(n_chunks,),
        in_specs=[grp_blk(1), grp_blk(2), grp_blk(3), grp_blk(4),
                  row_blk(HEAD_W), row_blk(HEAD_W),
                  whole((N_HEADS, HEAD_W, 2 * HEAD_W)), whole((1, HEAD_W)), whole((1, GROUP_W))],
        out_specs=[row_blk(GROUP_W), whole((N_HEADS, HEAD_W, 2 * HEAD_W)), whole((1, HEAD_W))],
        out_shape=[jax.ShapeDtypeStruct((n_rows, GROUP_W), BF16),
                   jax.ShapeDtypeStruct((N_HEADS, HEAD_W, 2 * HEAD_W), F32),
                   jax.ShapeDtypeStruct((1, HEAD_W), F32)],
        scratch_shapes=[pltpu.VMEM((N_HEADS, HEAD_W, 2 * HEAD_W), F32), pltpu.VMEM((1, HEAD_W), F32)],
        compiler_params=_cparams(1),
        name="mlstm_prompt",
    )(proj, proj, proj, proj, ig, lf, c0, m0, g_ml)


def _decode_kernel(pt_ref, q_ref, kn_ref, vn_ref, lam_ref, g_ref, *rest, pages_per_step, n_steps):
    k_refs = rest[:pages_per_step]
    v_refs = rest[pages_per_step:2 * pages_per_step]
    o_ref, m_sc, l_sc, acc_sc = rest[2 * pages_per_step:]
    j = pl.program_id(1)
    n_rows = 2 * N_HEADS
    page_rows = PAGE_SIZE * N_HEADS
    q16 = q_ref[0]
    rr = lax.broadcasted_iota(jnp.int32, (n_rows, HEAD_W), 0)
    cc = lax.broadcasted_iota(jnp.int32, (n_rows, HEAD_W), 1)
    qm = jnp.where(cc // DA_QK == rr // N_HEADS, q16, 0.0)

    @pl.when(j == 0)
    def _():
        m_sc[...] = jnp.broadcast_to(jnp.sum(qm * kn_ref[0], axis=1, keepdims=True), m_sc.shape)
        l_sc[...] = jnp.ones_like(l_sc)
        acc_sc[...] = vn_ref[0]

    qb = qm.astype(BF16)
    key_head = lax.broadcasted_iota(jnp.int32, (n_rows, page_rows), 1) % N_HEADS
    row_head = lax.broadcasted_iota(jnp.int32, (n_rows, page_rows), 0) % N_HEADS
    own = key_head == row_head
    s_pages = [jnp.where(own, lax.dot_general(qb, kr[0].astype(BF16), _NT, preferred_element_type=F32),
                         NEG_INF) for kr in k_refs]
    s_max = s_pages[0]
    for s in s_pages[1:]:
        s_max = jnp.maximum(s_max, s)
    m_prev = m_sc[...]
    m_new = jnp.maximum(m_prev, jnp.max(s_max, axis=1, keepdims=True))
    alpha = jnp.exp2(m_prev - m_new)
    m_rep = jnp.concatenate([m_new] * (page_rows // HEAD_W), axis=1)
    l_add = jnp.zeros((n_rows, 1), F32)
    pv = jnp.zeros((n_rows, HEAD_W), F32)
    for s, vr in zip(s_pages, v_refs):
        p = jnp.exp2(s - m_rep)
        l_add += jnp.sum(p, axis=1, keepdims=True)
        pv += jnp.dot(p.astype(BF16), vr[0].astype(BF16), preferred_element_type=F32)
    l_sc[...] = alpha * l_sc[...] + l_add
    acc_sc[...] = alpha * acc_sc[...] + pv
    m_sc[...] = m_new

    @pl.when(j == n_steps - 1)
    def _():
        o = acc_sc[...] / l_sc[...]
        lam = (jnp.exp(jnp.sum(lam_ref[0:1, :] * lam_ref[1:2, :], axis=1, keepdims=True))
               - jnp.exp(jnp.sum(lam_ref[2:3, :] * lam_ref[3:4, :], axis=1, keepdims=True)) + LAM_INIT)
        d = o[:N_HEADS] - lam * o[N_HEADS:]
        y = d * lax.rsqrt(jnp.mean(d * d, axis=-1, keepdims=True) + EPS) * g_ref[...]
        o_ref[0] = y * (1.0 - LAM_INIT)


def _decode_attn(page_table, q, k_new, v_new, cache_k, cache_v, lam4, g_da):
    b, n_pages = page_table.shape
    pps = _pick(n_pages, 8)
    n_steps = n_pages // pps
    pt_flat = page_table.reshape(-1)
    n_rows = 2 * N_HEADS

    def page_spec(g):
        return pl.BlockSpec((1, PAGE_SIZE * N_HEADS, HEAD_W),
                            lambda bi, j, pt: (pt[bi * n_pages + j * pps + g], 0, 0))

    seq_spec = pl.BlockSpec((1, n_rows, HEAD_W), lambda bi, j, pt: (bi, 0, 0))
    grid_spec = pltpu.PrefetchScalarGridSpec(
        num_scalar_prefetch=1,
        grid=(b, n_steps),
        in_specs=[seq_spec, seq_spec, seq_spec,
                  pl.BlockSpec((4, DA_QK), lambda bi, j, pt: (0, 0)),
                  pl.BlockSpec((1, HEAD_W), lambda bi, j, pt: (0, 0))]
                 + [page_spec(g) for g in range(pps)] + [page_spec(g) for g in range(pps)],
        out_specs=pl.BlockSpec((1, N_HEADS, HEAD_W), lambda bi, j, pt: (bi, 0, 0)),
        scratch_shapes=[pltpu.VMEM((n_rows, HEAD_W), F32), pltpu.VMEM((n_rows, HEAD_W), F32),
                        pltpu.VMEM((n_rows, HEAD_W), F32)])
    return pl.pallas_call(
        functools.partial(_decode_kernel, pages_per_step=pps, n_steps=n_steps),
        grid_spec=grid_spec,
        out_shape=jax.ShapeDtypeStruct((b, N_HEADS, HEAD_W), F32),
        compiler_params=_cparams(2),
        name="diff_attn_decode",
    )(pt_flat, q, k_new, v_new, lam4, g_da, *([cache_k] * pps), *([cache_v] * pps))


def _mlstm_step_kernel(q_ref, k_ref, v_ref, o_ref, ig_ref, lf_ref, c_ref, n_ref, m_ref, g_ref,
                       y_ref, c_out_ref, n_out_ref, m_out_ref):
    q8 = q_ref[0]
    k8 = k_ref[0]
    v8 = v_ref[0]
    n8 = n_ref[0]
    ig = ig_ref[0]
    lf = lf_ref[0]
    m = m_ref[0]
    inter = lf + m
    m_t = jnp.maximum(inter, ig)
    w_all = jnp.exp(ig - m_t)
    g_all = jnp.exp(inter - m_t)
    floor_all = jnp.exp(-m_t)
    m_out_ref[0] = m_t
    q_t = q8.T
    k_t = k8.T
    qk = jnp.sum(q8 * k8, axis=1, keepdims=True)
    qn = jnp.sum(q8 * n8, axis=1, keepdims=True)
    for h in range(N_HEADS):
        w = w_all[:, h:h + 1]
        g = g_all[:, h:h + 1]
        c_h = c_ref[0, h]
        vh = v8[h:h + 1]
        a = qk[h:h + 1] * w
        q_c = jnp.sum(q_t[:, h:h + 1] * c_h, axis=0, keepdims=True)
        num = a * vh + g * q_c
        den = a + g * qn[h:h + 1]
        hh = num / jnp.maximum(jnp.abs(den), floor_all[:, h:h + 1])
        y = hh * lax.rsqrt(jnp.mean(hh * hh, axis=-1, keepdims=True) + EPS) * g_ref[h:h + 1]
        y_ref[0, h:h + 1] = (y * jax.nn.sigmoid(o_ref[0, h:h + 1].astype(F32))).astype(y_ref.dtype)
        c_out_ref[0, h] = g * c_h + w * (k_t[:, h:h + 1] * vh)
        n_out_ref[0, h:h + 1] = g * n8[h:h + 1] + w * k8[h:h + 1]


def _mlstm_step(q, k, v, o, ig, lf, c, n, m, g_ml):
    b = q.shape[0]
    hv = pl.BlockSpec((1, N_HEADS, HEAD_W), lambda i: (i, 0, 0))
    gate = pl.BlockSpec((1, 1, HEAD_W), lambda i: (i, 0, 0))
    cs = pl.BlockSpec((1, N_HEADS, HEAD_W, HEAD_W), lambda i: (i, 0, 0, 0))
    return pl.pallas_call(
        _mlstm_step_kernel,
        grid=(b,),
        in_specs=[hv, hv, hv, hv, gate, gate, cs, hv, gate,
                  pl.BlockSpec((N_HEADS, HEAD_W), lambda i: (0, 0))],
        out_specs=[hv, cs, hv, gate],
        out_shape=[jax.ShapeDtypeStruct((b, N_HEADS, HEAD_W), F32),
                   jax.ShapeDtypeStruct(c.shape, F32),
                   jax.ShapeDtypeStruct((b, N_HEADS, HEAD_W), F32),
                   jax.ShapeDtypeStruct((b, 1, HEAD_W), F32)],
        compiler_params=_cparams(1),
        name="mlstm_sample",
    )(q, k, v, o, ig, lf, c, n, m, g_ml)


def _out_proj_kernel(da_ref, ml_ref, x_ref, w_ref, o_ref, wb_ref):
    @pl.when(pl.program_id(1) == 0)
    def _():
        wb_ref[...] = w_ref[...].astype(BF16)

    y = jnp.concatenate([da_ref[...], ml_ref[...]], axis=1)
    o_ref[...] = x_ref[...] + jnp.dot(y, wb_ref[...], preferred_element_type=F32)


def _out_proj(da, ml, x, w_out):
    m, d = x.shape
    tm = _pick(m, 1024)
    tn = _pick(d, 1024)
    return pl.pallas_call(
        _out_proj_kernel,
        grid=(d // tn, m // tm),
        in_specs=[pl.BlockSpec((tm, GROUP_W), lambda j, i: (i, 0)),
                  pl.BlockSpec((tm, GROUP_W), lambda j, i: (i, 0)),
                  pl.BlockSpec((tm, tn), lambda j, i: (i, j)),
                  pl.BlockSpec((2 * GROUP_W, tn), lambda j, i: (0, j))],
        out_specs=pl.BlockSpec((tm, tn), lambda j, i: (i, j)),
        out_shape=jax.ShapeDtypeStruct((m, d), F32),
        scratch_shapes=[pltpu.VMEM((2 * GROUP_W, tn), BF16)],
        compiler_params=_cparams(2),
        name="out_proj",
    )(da, ml, x, w_out)


def _ffn_kernel(*refs, sequential, tm, tf, n_f):
    if sequential:
        (x_ref, gf_ref, gl_ref, wa_ref, wg_ref, wd_ref, cwa_ref, cwg_ref, cba_ref, cbg_ref,
         pa_ref, pg_ref, y_ref, ua_ref, ug_ref, h_sc, acc_sc, carry_sc, u_sc) = refs
    else:
        (x_ref, gf_ref, gl_ref, wa_ref, wg_ref, wd_ref, cwa_ref, cwg_ref, cba_ref, cbg_ref,
         s0a_ref, s1a_ref, s0g_ref, s1g_ref, y_ref, ua_ref, ug_ref, wab_ref, wgb_ref, wdb_ref,
         h_sc, acc_sc) = refs
        wab_ref[...] = wa_ref[...].astype(BF16)
        wgb_ref[...] = wg_ref[...].astype(BF16)
        wdb_ref[...] = wd_ref[...].astype(BF16)
        wa_ref, wg_ref, wd_ref = wab_ref, wgb_ref, wdb_ref
    t = pl.program_id(0)
    f = pl.program_id(1)
    w_sub = min(tf, 256)
    n_c = tf // w_sub

    @pl.when(f == 0)
    def _():
        x = x_ref[...]
        h_sc[...] = (x * lax.rsqrt(jnp.mean(x * x, axis=-1, keepdims=True) + EPS)
                     * gf_ref[...]).astype(BF16)
        acc_sc[...] = jnp.zeros_like(acc_sc)

    h = h_sc[...]

    if sequential:
        for which, p_ref in enumerate((pa_ref, pg_ref)):
            @pl.when(t == 0)
            def _():
                carry_sc[which, f] = p_ref[...]

            u_sc[which, 0:8, :] = carry_sc[which, f]

    def up(c):
        cs = slice(c * w_sub, (c + 1) * w_sub)
        return (jnp.dot(h, wa_ref[:, cs], preferred_element_type=F32),
                jnp.dot(h, wg_ref[:, cs], preferred_element_type=F32))

    def conv(u, which, c, cw_ref, cb_ref, u_out_ref):
        cs = slice(c * w_sub, (c + 1) * w_sub)
        if sequential:
            u_sc[which, 8:8 + tm, cs] = u
            s1 = u_sc[which, 7:7 + tm, cs]
            s0 = u_sc[which, 6:6 + tm, cs]
        else:
            s0 = (s0a_ref, s0g_ref)[which][:, cs]
            s1 = (s1a_ref, s1g_ref)[which][:, cs]
            u_out_ref[:, cs] = u
        return cb_ref[:, cs] + cw_ref[0:1, cs] * s0 + cw_ref[1:2, cs] * s1 + cw_ref[2:3, cs] * u

    pending = [up(0)]
    down = None
    for c in range(n_c):
        u_a, u_g = pending.pop(0)
        if c + 1 < n_c:
            pending.append(up(c + 1))
        ya = conv(u_a, 0, c, cwa_ref, cba_ref, ua_ref)
        yg = conv(u_g, 1, c, cwg_ref, cbg_ref, ug_ref)
        act = (ya * jax.nn.sigmoid(ya) * yg).astype(BF16)
        part = jnp.dot(act, wd_ref[c * w_sub:(c + 1) * w_sub, :], preferred_element_type=F32)
        down = part if down is None else down + part
    acc_sc[...] += down

    if sequential:
        for which, u_out_ref in enumerate((ua_ref, ug_ref)):
            tail8 = u_sc[which, tm:tm + 8, :]
            carry_sc[which, f] = tail8
            u_out_ref[...] = tail8

    @pl.when(f == n_f - 1)
    def _():
        x2 = x_ref[...] + acc_sc[...]
        y_ref[...] = x2 * lax.rsqrt(jnp.mean(x2 * x2, axis=-1, keepdims=True) + EPS) * gl_ref[...]


def _ffn(x, g_ffn, g_final, w_a, w_g, w_down, conv_w, conv_b, *, prefix=None, s0=None, s1=None):
    m, d = x.shape
    d_ff = w_down.shape[0]
    sequential = prefix is not None
    tm = _pick(m, 512)
    tf = _pick(d_ff, 512)
    n_f = d_ff // tf
    n_t = m // tm
    g_off = 0 if sequential else n_f
    col_a = lambda r, w: pl.BlockSpec((r, w), lambda t, f: (0, f))
    col_g = lambda r, w: pl.BlockSpec((r, w), lambda t, f: (0, f + n_f))
    wd_spec = pl.BlockSpec((tf, d), lambda t, f: (f, 0))
    in_specs = [pl.BlockSpec((tm, d), lambda t, f: (t, 0)),
                pl.BlockSpec((1, d), lambda t, f: (0, 0)),
                pl.BlockSpec((1, d), lambda t, f: (0, 0)),
                col_a(d, tf), pl.BlockSpec((d, tf), lambda t, f: (0, f + g_off)), wd_spec,
                col_a(CONV_W, tf), col_g(CONV_W, tf), col_a(1, tf), col_g(1, tf)]
    args = [x, g_ffn, g_final, w_a, w_g, w_down, conv_w, conv_w, conv_b, conv_b]
    scratch = [pltpu.VMEM((tm, d), BF16), pltpu.VMEM((tm, d), F32)]
    out_specs = [pl.BlockSpec((tm, d), lambda t, f: (t, 0))]
    out_shape = [jax.ShapeDtypeStruct((m, d), F32)]
    if sequential:
        in_specs += [col_a(8, tf), col_g(8, tf)]
        args += [prefix, prefix]
        u_rows = 8 * n_t
        u_spec = pl.BlockSpec((8, tf), lambda t, f: (t, f))
        scratch += [pltpu.VMEM((2, n_f, 8, tf), F32), pltpu.VMEM((2, tm + 8, tf), F32)]
    else:
        assert n_t == 1, "the bf16 weight outputs are written once per weight block"
        rows_a = pl.BlockSpec((tm, tf), lambda t, f: (t, f))
        rows_g = pl.BlockSpec((tm, tf), lambda t, f: (t, f + n_f))
        in_specs += [rows_a, rows_a, rows_g, rows_g]
        args += [s0, s1, s0, s1]
        u_rows = m
        u_spec = rows_a
    out_specs += [u_spec, u_spec]
    out_shape += [jax.ShapeDtypeStruct((u_rows, d_ff), F32)] * 2
    if not sequential:
        out_specs += [col_a(d, tf), col_a(d, tf), wd_spec]
        out_shape += [jax.ShapeDtypeStruct((d, d_ff), BF16)] * 2 + [jax.ShapeDtypeStruct((d_ff, d), BF16)]
    y, ua, ug, *w_bf16 = pl.pallas_call(
        functools.partial(_ffn_kernel, sequential=sequential, tm=tm, tf=tf, n_f=n_f),
        grid=(n_t, n_f),
        in_specs=in_specs,
        out_specs=out_specs,
        out_shape=out_shape,
        scratch_shapes=scratch,
        compiler_params=_cparams(2),
        name="conv_ffn_seq" if sequential else "conv_ffn_rows",
    )(*args)
    return y, jnp.concatenate([ua, ug], axis=1), w_bf16 or None


def kernel(x_prompt, x_sample, cache_k, cache_v, page_table, state_C, state_n, state_m, state_conv,
           meta_tokens, g_mix, w_in, b_i, b_f, lam_q1, lam_k1, lam_q2, lam_k2, g_da, g_ml, w_out,
           g_ffn, w_up, conv_w, conv_b, w_down, g_final):
    depth = w_in.shape[0]
    assert depth == 1, "single-layer trunk"
    assert x_prompt.shape[0] == 1 and x_sample.shape[1] == 1
    seq, d = x_prompt.shape[1:]
    n_dec = x_sample.shape[0]
    assert N_META + n_dec <= TAIL_ROWS
    d_ff = w_down.shape[1]
    n_pool = cache_k.shape[1]
    gate_off = 7 * GROUP_W
    s_lo, s_hi = N_META, N_META + n_dec

    x_main = x_prompt[0]
    x_tail = jnp.concatenate([meta_tokens.astype(F32), x_sample[:, 0],
                              jnp.zeros((TAIL_ROWS - s_hi, d), F32)], axis=0)
    w_in0 = w_in[0]
    wg = jnp.zeros((d, 2 * HEAD_W), F32)
    wg = wg.at[:, :N_HEADS].set(w_in0[:, gate_off:gate_off + N_HEADS])
    wg = wg.at[:, HEAD_W:HEAD_W + N_HEADS].set(w_in0[:, gate_off + N_HEADS:gate_off + 2 * N_HEADS])
    wg = wg.astype(BF16)
    gbias = jnp.zeros((1, 2 * HEAD_W), F32)
    gbias = gbias.at[0, :N_HEADS].set(b_i[0]).at[0, HEAD_W:HEAD_W + N_HEADS].set(b_f[0])
    lam4 = jnp.stack([lam_q1[0], lam_k1[0], lam_q2[0], lam_k2[0]]).astype(F32)
    g_da2 = g_da[0].reshape(1, HEAD_W)
    g_ml2 = g_ml[0].reshape(1, GROUP_W)
    g_mix2 = g_mix[0].reshape(1, d)
    g_ffn2 = g_ffn[0].reshape(1, d)
    g_fin2 = g_final.reshape(1, d)
    conv_w2 = conv_w[0]
    conv_b2 = conv_b[0].reshape(1, 2 * d_ff)
    k_scale = HEAD_W ** -0.5
    q_scale = DA_QK ** -0.5 * LOG2E

    def project(x):
        h, ig, lf = _pre(x, g_mix2, wg, gbias)
        kv_f, kv_b = _proj(h, w_in0, (1, 2), (1.0, 1.0), True)
        oth = _proj(h, w_in0, (0, 3, 4, 5, 6), (q_scale, 1.0, k_scale, 1.0, 1.0), False)
        return dict(kv_f=kv_f, kv_b=kv_b, oth=oth, ig=ig, lf=lf)

    pm = project(x_main)
    pt = project(x_tail)
    pt.update(k_f=pt["kv_f"][0], v_f=pt["kv_f"][1], dq=pt["oth"][0], mq=pt["oth"][1],
              mk=pt["oth"][2], mv=pt["oth"][3], mo=pt["oth"][4])

    tk = _pick(seq, 512)
    tq = _pick(seq, 1024)
    pairs = [(i, j) for i in range(seq // tq) for j in range((i + 1) * tq // tk)]
    kv_main = ((pm["kv_b"], 0), (pm["kv_b"], 1))
    kv_meta = ((pt["kv_b"], 0), (pt["kv_b"], 1))
    da_main = _attn((pm["oth"], 0), *kv_main, *kv_meta, lam4, g_da2,
                    n_q_rows=seq, tq=tq, tk=tk, q_pos0=0, prefix_off=FAR, pairs=pairs)
    da_meta = _attn((pt["oth"], 0), *kv_main, *kv_meta, lam4, g_da2,
                    n_q_rows=N_META, tq=N_META, tk=tk, q_pos0=-FAR, prefix_off=0, pairs=[(0, 0)])

    c_zero = jnp.zeros((N_HEADS, HEAD_W, 2 * HEAD_W), F32)
    m_zero = jnp.zeros((1, HEAD_W), F32)
    ml_meta, c_meta, m_meta = _mlstm(pt["oth"], pt["ig"], pt["lf"], c_zero, m_zero, g_ml2,
                                     n_rows=N_META, chunk=N_META)
    ml_main, c_fin, m_fin = _mlstm(pm["oth"], pm["ig"], pm["lf"], c_meta, m_meta, g_ml2,
                                   n_rows=seq, chunk=_pick(seq, 256))

    f32_rows = lambda a: a[s_lo:s_hi].astype(F32)
    heads = lambda a: a.reshape(n_dec, N_HEADS, HEAD_W)
    twice = lambda a: jnp.concatenate([heads(a), heads(a)], axis=1)
    ck = cache_k[0].reshape(n_pool, PAGE_SIZE * N_HEADS, HEAD_W)
    cv = cache_v[0].reshape(n_pool, PAGE_SIZE * N_HEADS, HEAD_W)
    da_smp = _decode_attn(page_table, twice(f32_rows(pt["dq"])), twice(pt["k_f"][s_lo:s_hi]),
                          twice(pt["v_f"][s_lo:s_hi]), ck, cv, lam4, g_da2)
    da_smp = da_smp.reshape(n_dec, GROUP_W).astype(BF16)
    m_pad = jnp.zeros((n_dec, 1, HEAD_W), F32).at[:, 0, :N_HEADS].set(state_m[0].astype(F32))
    ml_smp, c_smp, n_smp, m_smp = _mlstm_step(
        heads(f32_rows(pt["mq"])), heads(f32_rows(pt["mk"])), heads(f32_rows(pt["mv"])),
        heads(pt["mo"][s_lo:s_hi]), pt["ig"][s_lo:s_hi, None, :], pt["lf"][s_lo:s_hi, None, :],
        state_C[0].astype(F32), state_n[0].astype(F32), m_pad, g_ml2.reshape(N_HEADS, HEAD_W))
    ml_smp = ml_smp.reshape(n_dec, GROUP_W).astype(BF16)

    pad_rows = jnp.zeros((TAIL_ROWS - s_hi, GROUP_W), BF16)
    da_tail = jnp.concatenate([da_meta, da_smp, pad_rows], axis=0)
    ml_tail = jnp.concatenate([ml_meta, ml_smp, pad_rows], axis=0)
    x1_main = _out_proj(da_main, ml_main, x_main, w_out[0])
    x1_tail = _out_proj(da_tail, ml_tail, x_tail, w_out[0])

    hist = jnp.zeros((2, TAIL_ROWS, 2 * d_ff), F32).at[:, s_lo:s_hi].set(
        jnp.swapaxes(state_conv[0].astype(F32), 0, 1))
    y_tail, u_tail, (wa_b, wg_b, wd_b) = _ffn(x1_tail, g_ffn2, g_fin2, w_up[0], w_up[0], w_down[0],
                                              conv_w2, conv_b2, s0=hist[0], s1=hist[1])
    y_main, u_last, _ = _ffn(x1_main, g_ffn2, g_fin2, wa_b, wg_b, wd_b, conv_w2, conv_b2,
                             prefix=u_tail[N_META - 8:N_META])

    def rows5(meta_part, main_part):
        return jnp.concatenate([meta_part, main_part], axis=0).reshape(
            1, 1, N_META + seq, N_HEADS, HEAD_W)

    y_prompt = y_main[None]
    y_sample = y_tail[s_lo:s_hi, None, :]
    k_rows_p = rows5(pt["k_f"][:N_META], pm["kv_f"][0])
    v_rows_p = rows5(pt["v_f"][:N_META], pm["kv_f"][1])
    k_rows_s = pt["k_f"][s_lo:s_hi].reshape(1, n_dec, 1, N_HEADS, HEAD_W)
    v_rows_s = pt["v_f"][s_lo:s_hi].reshape(1, n_dec, 1, N_HEADS, HEAD_W)
    c_p = c_fin[None, None, :, :, :HEAD_W]
    n_p = c_fin[None, None, :, :, HEAD_W]
    m_p = m_fin[None, :, :N_HEADS]
    c_s = c_smp[None]
    n_s = n_smp[None]
    m_s = m_smp[None, :, 0, :N_HEADS]
    conv_p = u_last[None, None, -2:, :]
    conv_s = jnp.stack([state_conv[0][:, 1].astype(F32), u_tail[s_lo:s_hi]], axis=1)[None]
    return (y_prompt, y_sample, k_rows_p, v_rows_p, k_rows_s, v_rows_s,
            c_p, n_p, m_p, c_s, n_s, m_s, conv_p, conv_s)
```

```python
import functools

import jax
import jax.numpy as jnp
from jax import lax
from jax.experimental import pallas as pl
from jax.experimental.pallas import tpu as pltpu

F32 = jnp.float32
BF16 = jnp.bfloat16

N_META = 16
N_HEADS = 8
HEAD_W = 128
DA_QK = 64
PAGE_SIZE = 128
CONV_W = 3
EPS = 1e-6
GROUP_W = N_HEADS * HEAD_W
LAM_INIT = 0.8 - 0.6 * 1.0
TAIL_ROWS = 128
NEG_INF = float("-inf")
LOG2E = 1.4426950408889634
FAR = 1 << 24

VMEM_LIMIT = 52 * 1024 * 1024

_NT = (((1,), (1,)), ((), ()))


def _cparams(n_axes):
    return pltpu.CompilerParams(dimension_semantics=("arbitrary",) * n_axes,
                                vmem_limit_bytes=VMEM_LIMIT)


def _pick(total, pref):
    t = min(pref, total)
    while total % t:
        t //= 2
    return t


def _pre_kernel(x_ref, g_ref, wg_ref, bias_ref, h_ref, ig_ref, lf_ref):
    x = x_ref[...]
    y = x * lax.rsqrt(jnp.mean(x * x, axis=-1, keepdims=True) + EPS) * g_ref[...]
    hb = y.astype(BF16)
    h_ref[...] = hb
    z = lax.dot_general(hb, wg_ref[...], _NT, preferred_element_type=F32) + bias_ref[...]
    ig_ref[...] = z[:, :HEAD_W]
    zf = z[:, HEAD_W:]
    lf_ref[...] = jnp.minimum(zf, 0.0) - jnp.log1p(jnp.exp(-jnp.abs(zf)))


def _pre(x, g, wg, bias):
    m, d = x.shape
    tm = _pick(m, 512)
    return pl.pallas_call(
        _pre_kernel,
        grid=(m // tm,),
        in_specs=[pl.BlockSpec((tm, d), lambda i: (i, 0)),
                  pl.BlockSpec((1, d), lambda i: (0, 0)),
                  pl.BlockSpec((2 * HEAD_W, d), lambda i: (0, 0)),
                  pl.BlockSpec((1, 2 * HEAD_W), lambda i: (0, 0))],
        out_specs=[pl.BlockSpec((tm, d), lambda i: (i, 0)),
                   pl.BlockSpec((tm, HEAD_W), lambda i: (i, 0)),
                   pl.BlockSpec((tm, HEAD_W), lambda i: (i, 0))],
        out_shape=[jax.ShapeDtypeStruct((m, d), BF16),
                   jax.ShapeDtypeStruct((m, HEAD_W), F32),
                   jax.ShapeDtypeStruct((m, HEAD_W), F32)],
        compiler_params=_cparams(1),
        name="pre_norm_gates",
    )(x, g, wg, bias)


def _proj_kernel(h_ref, w_ref, s_ref, *rest, with_f32):
    if with_f32:
        of_ref, ob_ref, wb_ref = rest
    else:
        ob_ref, wb_ref = rest

    @pl.when(pl.program_id(1) == 0)
    def _():
        wb_ref[...] = w_ref[...].astype(BF16)

    z = lax.dot_general(h_ref[...], wb_ref[...], _NT, preferred_element_type=F32)
    if with_f32:
        for hd in range(N_HEADS):
            of_ref[0, :, hd, :] = z[:, hd * HEAD_W:(hd + 1) * HEAD_W]
    ob_ref[0] = (z * s_ref[0, :, 0:1]).astype(BF16)


def _proj(h, w_t, groups, scales, with_f32):
    m, d = h.shape
    tm = _pick(m, 1024)
    ng = len(groups)
    first, rest_off = groups[0], groups[1] - 1 if ng > 1 else 0
    assert list(groups) == [first] + [rest_off + k for k in range(1, ng)]
    s_arr = jnp.broadcast_to(jnp.asarray(scales, F32)[:, None, None], (ng, 1, HEAD_W))

    def w_map(j, i):
        return (jnp.where(j == 0, first, j + rest_off), 0)

    out_specs = [pl.BlockSpec((1, tm, GROUP_W), lambda j, i: (j, i, 0))]
    out_shape = [jax.ShapeDtypeStruct((ng, m, GROUP_W), BF16)]
    if with_f32:
        out_specs = [pl.BlockSpec((1, tm, N_HEADS, HEAD_W), lambda j, i: (j, i, 0, 0))] + out_specs
        out_shape = [jax.ShapeDtypeStruct((ng, m, N_HEADS, HEAD_W), F32)] + out_shape
    outs = pl.pallas_call(
        functools.partial(_proj_kernel, with_f32=with_f32),
        grid=(ng, m // tm),
        in_specs=[pl.BlockSpec((tm, d), lambda j, i: (i, 0)),
                  pl.BlockSpec((GROUP_W, d), w_map),
                  pl.BlockSpec((1, 1, HEAD_W), lambda j, i: (j, 0, 0))],
        out_specs=out_specs,
        out_shape=out_shape,
        scratch_shapes=[pltpu.VMEM((GROUP_W, d), BF16)],
        compiler_params=_cparams(2),
        name="in_proj_f32" if with_f32 else "in_proj",
    )(h, w_t, s_arr)
    return outs if with_f32 else outs[0]


def _proj_rows_kernel(h_ref, w_ref, meta_ref, rows_out, ob_ref, wb_ref, stage, sems, meta_sem,
                      *, tm, n_i, row0):
    i = pl.program_id(0)
    slot = i % 2

    def row_copy(sl, blk):
        return pltpu.make_async_copy(stage.at[sl], rows_out.at[pl.ds(row0 + blk * tm, tm)], sems.at[sl])

    @pl.when(i == 0)
    def _():
        wb_ref[...] = w_ref[...].astype(BF16)
        meta_copy = pltpu.make_async_copy(meta_ref, rows_out.at[pl.ds(0, row0)], meta_sem.at[0])
        meta_copy.start()
        meta_copy.wait()

    @pl.when(i >= 2)
    def _():
        row_copy(slot, i - 2).wait()

    z = lax.dot_general(h_ref[...], wb_ref[...], _NT, preferred_element_type=F32)
    for hd in range(N_HEADS):
        stage[slot, :, hd, :] = z[:, hd * HEAD_W:(hd + 1) * HEAD_W]
    ob_ref[...] = z.astype(BF16)
    row_copy(slot, i).start()

    @pl.when(i == n_i - 1)
    def _():
        row_copy(slot, i).wait()
        if n_i >= 2:
            row_copy(1 - slot, i - 1).wait()


def _proj_rows(h, w_t, group, meta_rows):
    m, d = h.shape
    row0 = meta_rows.shape[0]
    tm = _pick(m, 1024)
    n_i = m // tm
    return pl.pallas_call(
        functools.partial(_proj_rows_kernel, tm=tm, n_i=n_i, row0=row0),
        grid=(n_i,),
        in_specs=[pl.BlockSpec((tm, d), lambda i: (i, 0)),
                  pl.BlockSpec((GROUP_W, d), lambda i: (group, 0)),
                  pl.BlockSpec((row0, N_HEADS, HEAD_W), lambda i: (0, 0, 0))],
        out_specs=[pl.BlockSpec(memory_space=pl.ANY),
                   pl.BlockSpec((tm, GROUP_W), lambda i: (i, 0))],
        out_shape=[jax.ShapeDtypeStruct((row0 + m, N_HEADS, HEAD_W), F32),
                   jax.ShapeDtypeStruct((m, GROUP_W), BF16)],
        scratch_shapes=[pltpu.VMEM((GROUP_W, d), BF16),
                        pltpu.VMEM((2, tm, N_HEADS, HEAD_W), F32),
                        pltpu.SemaphoreType.DMA((2,)),
                        pltpu.SemaphoreType.DMA((1,))],
        compiler_params=_cparams(1),
        name="in_proj_rows",
    )(h, w_t, meta_rows)


def _attn_kernel(qi_ref, kj_ref, q_ref, k_ref, v_ref, kp_ref, vp_ref, lam_ref, g_ref,
                 o_ref, m_sc, l_sc, acc_sc, *, tq, tk, sub, ahead, q_pos0, prefix_off, n_steps,
                 mask_deltas):
    step = pl.program_id(1)
    i = qi_ref[step]
    j = kj_ref[step]
    q = q_ref[...]
    lane = lax.broadcasted_iota(jnp.int32, q.shape, 1)
    zero = jnp.zeros_like(q)
    q2 = jnp.concatenate([jnp.where(lane < DA_QK, q, zero), jnp.where(lane >= DA_QK, q, zero)], axis=0)
    n_sub = 2 * tq // sub

    @pl.when(j == 0)
    def _():
        sp = lax.dot_general(q2, kp_ref[...], _NT, preferred_element_type=F32)
        col = lax.broadcasted_iota(jnp.int32, sp.shape, 1)
        r = lax.broadcasted_iota(jnp.int32, sp.shape, 0)
        ok = (col < N_META) & (col <= jnp.where(r >= tq, r - tq, r) + prefix_off)
        sp = jnp.where(ok, sp, NEG_INF)
        m0 = jnp.max(sp, axis=1, keepdims=True)
        p = jnp.exp2(sp - m0)
        m_sc[...] = jnp.broadcast_to(m0, m_sc.shape)
        l_sc[...] = jnp.broadcast_to(jnp.sum(p, axis=1, keepdims=True), l_sc.shape)
        acc_sc[...] = jnp.dot(p.astype(BF16), vp_ref[...], preferred_element_type=F32)

    def block(delta):
        def first_visible(r):
            return FAR if delta is None else delta + (r * sub) % tq
        live = [r for r in range(n_sub) if first_visible(r) + sub - 1 >= 0]
        if not live:
            return
        k = k_ref[...]
        v = v_ref[...]
        rel = (lax.broadcasted_iota(jnp.int32, (sub, tk), 1)
               - lax.broadcasted_iota(jnp.int32, (sub, tk), 0))
        m_all, l_all, acc_all = m_sc[...], l_sc[...], acc_sc[...]
        new = []

        def scores(r):
            return lax.dot_general(q2[r * sub:(r + 1) * sub], k, _NT, preferred_element_type=F32)

        pending = [scores(r) for r in live[:ahead]]
        for n, r in enumerate(live):
            rows = slice(r * sub, (r + 1) * sub)
            s = pending.pop(0)
            if n + ahead < len(live):
                pending.append(scores(live[n + ahead]))
            if first_visible(r) < tk - 1:
                s = jnp.where(rel <= first_visible(r), s, NEG_INF)
            m_prev = m_all[rows]
            m_new = jnp.maximum(m_prev, jnp.max(s, axis=1, keepdims=True))
            alpha = jnp.exp2(m_prev - m_new)
            p = jnp.exp2(s - jnp.concatenate([m_new] * (tk // HEAD_W), axis=1))
            new.append((m_new, alpha * l_all[rows] + jnp.sum(p, axis=1, keepdims=True),
                        alpha * acc_all[rows] + jnp.dot(p.astype(BF16), v, preferred_element_type=F32)))
        for r, (m_new, l_new, acc_new) in zip(live, new):
            rows = slice(r * sub, (r + 1) * sub)
            m_sc[rows] = m_new
            l_sc[rows] = l_new
            acc_sc[rows] = acc_new

    delta_now = q_pos0 + i * tq - j * tk
    pl.when(delta_now >= tk - 1)(lambda: block(None))
    for delta in mask_deltas:
        pl.when(delta_now == delta)(functools.partial(block, delta))

    is_last = jnp.logical_or(step == n_steps - 1, qi_ref[step + 1] != i)

    @pl.when(is_last)
    def _():
        o = acc_sc[...] / l_sc[...]
        lam = (jnp.exp(jnp.sum(lam_ref[0:1, :] * lam_ref[1:2, :], axis=1, keepdims=True))
               - jnp.exp(jnp.sum(lam_ref[2:3, :] * lam_ref[3:4, :], axis=1, keepdims=True)) + LAM_INIT)
        d = o[:tq] - lam * o[tq:]
        y = d * lax.rsqrt(jnp.mean(d * d, axis=-1, keepdims=True) + EPS) * g_ref[...]
        o_ref[...] = (y * (1.0 - LAM_INIT)).astype(o_ref.dtype)


def _attn(q, k, v, kp, vp, lam4, g_da, *, n_q_rows, tq, tk, q_pos0, prefix_off, pairs):
    qi = jnp.asarray([p[0] for p in pairs] + [-1], jnp.int32)
    kj = jnp.asarray([p[1] for p in pairs] + [0], jnp.int32)
    n_steps = len(pairs)
    deltas = {q_pos0 + pi * tq - pj * tk for pi, pj in pairs}
    kern = functools.partial(_attn_kernel, tq=tq, tk=tk, sub=min(256, 2 * tq), ahead=2, q_pos0=q_pos0,
                             prefix_off=prefix_off, n_steps=n_steps,
                             mask_deltas=tuple(sorted(dl for dl in deltas if dl < tk - 1)))
    grid_spec = pltpu.PrefetchScalarGridSpec(
        num_scalar_prefetch=2,
        grid=(N_HEADS, n_steps),
        in_specs=[pl.BlockSpec((None, tq, HEAD_W), lambda h, s, qi, kj: (q[1], qi[s], h)),
                  pl.BlockSpec((None, tk, HEAD_W), lambda h, s, qi, kj: (k[1], kj[s], h)),
                  pl.BlockSpec((None, tk, HEAD_W), lambda h, s, qi, kj: (v[1], kj[s], h)),
                  pl.BlockSpec((None, TAIL_ROWS, HEAD_W), lambda h, s, qi, kj: (kp[1], 0, h)),
                  pl.BlockSpec((None, TAIL_ROWS, HEAD_W), lambda h, s, qi, kj: (vp[1], 0, h)),
                  pl.BlockSpec((4, DA_QK), lambda h, s, qi, kj: (0, 0)),
                  pl.BlockSpec((1, HEAD_W), lambda h, s, qi, kj: (0, 0))],
        out_specs=pl.BlockSpec((tq, HEAD_W), lambda h, s, qi, kj: (qi[s], h)),
        scratch_shapes=[pltpu.VMEM((2 * tq, HEAD_W), F32), pltpu.VMEM((2 * tq, HEAD_W), F32),
                        pltpu.VMEM((2 * tq, HEAD_W), F32)])
    return pl.pallas_call(
        kern, grid_spec=grid_spec,
        out_shape=jax.ShapeDtypeStruct((n_q_rows, GROUP_W), BF16),
        compiler_params=_cparams(2),
        name="diff_attn_prompt",
    )(qi, kj, q[0], k[0], v[0], kp[0], vp[0], lam4, g_da)


def _mlstm_kernel(q_ref, k_ref, v_ref, o_ref, ig_ref, lf_ref, c0_ref, m0_ref, g_ref,
                  y_ref, c_out_ref, m_out_ref, c_sc, m_sc, *, chunk, n_chunks):
    c = pl.program_id(0)

    @pl.when(c == 0)
    def _():
        c_sc[...] = c0_ref[...]
        m_sc[...] = m0_ref[...]

    L = chunk
    ig = ig_ref[...]
    lf = lf_ref[...]
    row = lax.broadcasted_iota(jnp.int32, (L, L), 0)
    col = lax.broadcasted_iota(jnp.int32, (L, L), 1)
    causal = col <= row
    b_col = jnp.dot(causal.astype(F32), lf, preferred_element_type=F32,
                    precision=lax.Precision.HIGHEST)
    b_row = b_col.T
    ig_row = ig.T
    m_prev_all = m_sc[...]
    lane1 = lax.broadcasted_iota(jnp.int32, (1, HEAD_W), 1)
    m_next_all = m_prev_all
    ones = jnp.ones((L, HEAD_W), BF16)

    for h in range(N_HEADS):
        hs = slice(h * HEAD_W, (h + 1) * HEAD_W)
        bc = b_col[:, h:h + 1]
        m_prev = m_prev_all[:, h:h + 1]
        dm = jnp.where(causal, bc - b_row[h:h + 1, :] + ig_row[h:h + 1, :], NEG_INF)
        inter = bc + m_prev
        m_t = jnp.maximum(inter, jnp.max(dm, axis=1, keepdims=True))
        w = jnp.exp(dm - m_t)
        g = jnp.exp(inter - m_t)
        qh = q_ref[:, hs]
        kh = k_ref[:, hs]
        v_aug = jnp.concatenate([v_ref[:, hs], ones], axis=1)
        s = lax.dot_general(qh, kh, _NT, preferred_element_type=F32)
        a = (s * w).astype(BF16)
        c_h = c_sc[h]
        num_aug = (jnp.dot(a, v_aug, preferred_element_type=F32)
                   + g * jnp.dot(qh, c_h.astype(BF16), preferred_element_type=F32))
        num = num_aug[:, :HEAD_W]
        den = num_aug[:, HEAD_W:]
        hh = num / jnp.maximum(jnp.abs(den), jnp.exp(-m_t))
        y = hh * lax.rsqrt(jnp.mean(hh * hh, axis=-1, keepdims=True) + EPS) * g_ref[:, hs]
        y_ref[:, hs] = (y * jax.nn.sigmoid(o_ref[:, hs].astype(F32))).astype(y_ref.dtype)
        b_last = bc[L - 1:L]
        m_new = m_t[L - 1:L]
        ws = jnp.exp(b_last - bc + ig[:, h:h + 1] - m_new)
        decay = jnp.exp(b_last + m_prev - m_new)
        kw_t = (kh.astype(F32) * ws).T.astype(BF16)
        c_sc[h] = decay * c_h + jnp.dot(kw_t, v_aug, preferred_element_type=F32)
        m_next_all = jnp.where(lane1 == h, m_new, m_next_all)

    m_sc[...] = m_next_all

    @pl.when(c == n_chunks - 1)
    def _():
        c_out_ref[...] = c_sc[...]
        m_out_ref[...] = m_sc[...]


def _mlstm(proj, ig, lf, c0, m0, g_ml, *, n_rows, chunk):
    n_chunks = n_rows // chunk
    row_blk = lambda w: pl.BlockSpec((chunk, w), lambda c: (c, 0))
    grp_blk = lambda g: pl.BlockSpec((None, chunk, GROUP_W), lambda c: (g, c, 0))
    whole = lambda shape: pl.BlockSpec(shape, lambda c: (0,) * len(shape))
    return pl.pallas_call(
        functools.partial(_mlstm_kernel, chunk=chunk, n_chunks=n_chunks),
        grid=(n_chunks,),
        in_specs=[grp_blk(1), grp_blk(2), grp_blk(3), grp_blk(4),
                  row_blk(HEAD_W), row_blk(HEAD_W),
                  whole((N_HEADS, HEAD_W, 2 * HEAD_W)), whole((1, HEAD_W)), whole((1, GROUP_W))],
        out_specs=[row_blk(GROUP_W), whole((N_HEADS, HEAD_W, 2 * HEAD_W)), whole((1, HEAD_W))],
        out_shape=[jax.ShapeDtypeStruct((n_rows, GROUP_W), BF16),
                   jax.ShapeDtypeStruct((N_HEADS, HEAD_W, 2 * HEAD_W), F32),
                   jax.ShapeDtypeStruct((1, HEAD_W), F32)],
        scratch_shapes=[pltpu.VMEM((N_HEADS, HEAD_W, 2 * HEAD_W), F32), pltpu.VMEM((1, HEAD_W), F32)],
        compiler_params=_cparams(1),
        name="mlstm_prompt",
    )(proj, proj, proj, proj, ig, lf, c0, m0, g_ml)


def _decode_kernel(pt_ref, q_ref, kn_ref, vn_ref, lam_ref, g_ref, *rest, pages_per_step, n_steps):
    k_refs = rest[:pages_per_step]
    v_refs = rest[pages_per_step:2 * pages_per_step]
    o_ref, m_sc, l_sc, acc_sc = rest[2 * pages_per_step:]
    j = pl.program_id(1)
    n_rows = 2 * N_HEADS
    page_rows = PAGE_SIZE * N_HEADS
    q16 = q_ref[0]
    rr = lax.broadcasted_iota(jnp.int32, (n_rows, HEAD_W), 0)
    cc = lax.broadcasted_iota(jnp.int32, (n_rows, HEAD_W), 1)
    qm = jnp.where(cc // DA_QK == rr // N_HEADS, q16, 0.0)

    @pl.when(j == 0)
    def _():
        m_sc[...] = jnp.broadcast_to(jnp.sum(qm * kn_ref[0], axis=1, keepdims=True), m_sc.shape)
        l_sc[...] = jnp.ones_like(l_sc)
        acc_sc[...] = vn_ref[0]

    qb = qm.astype(BF16)
    key_head = lax.broadcasted_iota(jnp.int32, (n_rows, page_rows), 1) % N_HEADS
    row_head = lax.broadcasted_iota(jnp.int32, (n_rows, page_rows), 0) % N_HEADS
    own = key_head == row_head
    s_pages = [jnp.where(own, lax.dot_general(qb, kr[0].astype(BF16), _NT, preferred_element_type=F32),
                         NEG_INF) for kr in k_refs]
    s_max = s_pages[0]
    for s in s_pages[1:]:
        s_max = jnp.maximum(s_max, s)
    m_prev = m_sc[...]
    m_new = jnp.maximum(m_prev, jnp.max(s_max, axis=1, keepdims=True))
    alpha = jnp.exp2(m_prev - m_new)
    m_rep = jnp.concatenate([m_new] * (page_rows // HEAD_W), axis=1)
    l_add = jnp.zeros((n_rows, 1), F32)
    pv = jnp.zeros((n_rows, HEAD_W), F32)
    for s, vr in zip(s_pages, v_refs):
        p = jnp.exp2(s - m_rep)
        l_add += jnp.sum(p, axis=1, keepdims=True)
        pv += jnp.dot(p.astype(BF16), vr[0].astype(BF16), preferred_element_type=F32)
    l_sc[...] = alpha * l_sc[...] + l_add
    acc_sc[...] = alpha * acc_sc[...] + pv
    m_sc[...] = m_new

    @pl.when(j == n_steps - 1)
    def _():
        o = acc_sc[...] / l_sc[...]
        lam = (jnp.exp(jnp.sum(lam_ref[0:1, :] * lam_ref[1:2, :], axis=1, keepdims=True))
               - jnp.exp(jnp.sum(lam_ref[2:3, :] * lam_ref[3:4, :], axis=1, keepdims=True)) + LAM_INIT)
        d = o[:N_HEADS] - lam * o[N_HEADS:]
        y = d * lax.rsqrt(jnp.mean(d * d, axis=-1, keepdims=True) + EPS) * g_ref[...]
        o_ref[0] = y * (1.0 - LAM_INIT)


def _decode_attn(page_table, q, k_new, v_new, cache_k, cache_v, lam4, g_da):
    b, n_pages = page_table.shape
    pps = _pick(n_pages, 8)
    n_steps = n_pages // pps
    pt_flat = page_table.reshape(-1)
    n_rows = 2 * N_HEADS

    def page_spec(g):
        return pl.BlockSpec((1, PAGE_SIZE * N_HEADS, HEAD_W),
                            lambda bi, j, pt: (pt[bi * n_pages + j * pps + g], 0, 0))

    seq_spec = pl.BlockSpec((1, n_rows, HEAD_W), lambda bi, j, pt: (bi, 0, 0))
    grid_spec = pltpu.PrefetchScalarGridSpec(
        num_scalar_prefetch=1,
        grid=(b, n_steps),
        in_specs=[seq_spec, seq_spec, seq_spec,
                  pl.BlockSpec((4, DA_QK), lambda bi, j, pt: (0, 0)),
                  pl.BlockSpec((1, HEAD_W), lambda bi, j, pt: (0, 0))]
                 + [page_spec(g) for g in range(pps)] + [page_spec(g) for g in range(pps)],
        out_specs=pl.BlockSpec((1, N_HEADS, HEAD_W), lambda bi, j, pt: (bi, 0, 0)),
        scratch_shapes=[pltpu.VMEM((n_rows, HEAD_W), F32), pltpu.VMEM((n_rows, HEAD_W), F32),
                        pltpu.VMEM((n_rows, HEAD_W), F32)])
    return pl.pallas_call(
        functools.partial(_decode_kernel, pages_per_step=pps, n_steps=n_steps),
        grid_spec=grid_spec,
        out_shape=jax.ShapeDtypeStruct((b, N_HEADS, HEAD_W), F32),
        compiler_params=_cparams(2),
        name="diff_attn_decode",
    )(pt_flat, q, k_new, v_new, lam4, g_da, *([cache_k] * pps), *([cache_v] * pps))


def _mlstm_step_kernel(q_ref, k_ref, v_ref, o_ref, ig_ref, lf_ref, c_ref, n_ref, m_ref, g_ref,
                       y_ref, c_out_ref, n_out_ref, m_out_ref):
    q8 = q_ref[0]
    k8 = k_ref[0]
    v8 = v_ref[0]
    n8 = n_ref[0]
    ig = ig_ref[0]
    lf = lf_ref[0]
    m = m_ref[0]
    inter = lf + m
    m_t = jnp.maximum(inter, ig)
    w_all = jnp.exp(ig - m_t)
    g_all = jnp.exp(inter - m_t)
    floor_all = jnp.exp(-m_t)
    m_out_ref[0] = m_t
    q_t = q8.T
    k_t = k8.T
    qk = jnp.sum(q8 * k8, axis=1, keepdims=True)
    qn = jnp.sum(q8 * n8, axis=1, keepdims=True)
    for h in range(N_HEADS):
        w = w_all[:, h:h + 1]
        g = g_all[:, h:h + 1]
        c_h = c_ref[0, h]
        vh = v8[h:h + 1]
        a = qk[h:h + 1] * w
        q_c = jnp.sum(q_t[:, h:h + 1] * c_h, axis=0, keepdims=True)
        num = a * vh + g * q_c
        den = a + g * qn[h:h + 1]
        hh = num / jnp.maximum(jnp.abs(den), floor_all[:, h:h + 1])
        y = hh * lax.rsqrt(jnp.mean(hh * hh, axis=-1, keepdims=True) + EPS) * g_ref[h:h + 1]
        y_ref[0, h:h + 1] = (y * jax.nn.sigmoid(o_ref[0, h:h + 1].astype(F32))).astype(y_ref.dtype)
        c_out_ref[0, h] = g * c_h + w * (k_t[:, h:h + 1] * vh)
        n_out_ref[0, h:h + 1] = g * n8[h:h + 1] + w * k8[h:h + 1]


def _mlstm_step(q, k, v, o, ig, lf, c, n, m, g_ml):
    b = q.shape[0]
    hv = pl.BlockSpec((1, N_HEADS, HEAD_W), lambda i: (i, 0, 0))
    gate = pl.BlockSpec((1, 1, HEAD_W), lambda i: (i, 0, 0))
    cs = pl.BlockSpec((1, N_HEADS, HEAD_W, HEAD_W), lambda i: (i, 0, 0, 0))
    return pl.pallas_call(
        _mlstm_step_kernel,
        grid=(b,),
        in_specs=[hv, hv, hv, hv, gate, gate, cs, hv, gate,
                  pl.BlockSpec((N_HEADS, HEAD_W), lambda i: (0, 0))],
        out_specs=[hv, cs, hv, gate],
        out_shape=[jax.ShapeDtypeStruct((b, N_HEADS, HEAD_W), F32),
                   jax.ShapeDtypeStruct(c.shape, F32),
                   jax.ShapeDtypeStruct((b, N_HEADS, HEAD_W), F32),
                   jax.ShapeDtypeStruct((b, 1, HEAD_W), F32)],
        compiler_params=_cparams(1),
        name="mlstm_sample",
    )(q, k, v, o, ig, lf, c, n, m, g_ml)


def _out_proj_kernel(da_ref, ml_ref, x_ref, w_ref, o_ref, wb_ref):
    @pl.when(pl.program_id(1) == 0)
    def _():
        wb_ref[...] = w_ref[...].astype(BF16)

    y = jnp.concatenate([da_ref[...], ml_ref[...]], axis=1)
    o_ref[...] = x_ref[...] + jnp.dot(y, wb_ref[...], preferred_element_type=F32)


def _out_proj(da, ml, x, w_out):
    m, d = x.shape
    tm = _pick(m, 1024)
    tn = _pick(d, 1024)
    return pl.pallas_call(
        _out_proj_kernel,
        grid=(d // tn, m // tm),
        in_specs=[pl.BlockSpec((tm, GROUP_W), lambda j, i: (i, 0)),
                  pl.BlockSpec((tm, GROUP_W), lambda j, i: (i, 0)),
                  pl.BlockSpec((tm, tn), lambda j, i: (i, j)),
                  pl.BlockSpec((2 * GROUP_W, tn), lambda j, i: (0, j))],
        out_specs=pl.BlockSpec((tm, tn), lambda j, i: (i, j)),
        out_shape=jax.ShapeDtypeStruct((m, d), F32),
        scratch_shapes=[pltpu.VMEM((2 * GROUP_W, tn), BF16)],
        compiler_params=_cparams(2),
        name="out_proj",
    )(da, ml, x, w_out)


def _ffn_kernel(*refs, sequential, tm, tf, n_f):
    if sequential:
        (x_ref, gf_ref, gl_ref, wa_ref, wg_ref, wd_ref, cwa_ref, cwg_ref, cba_ref, cbg_ref,
         pa_ref, pg_ref, y_ref, ua_ref, ug_ref, h_sc, acc_sc, carry_sc, u_sc) = refs
    else:
        (x_ref, gf_ref, gl_ref, wa_ref, wg_ref, wd_ref, cwa_ref, cwg_ref, cba_ref, cbg_ref,
         s0a_ref, s1a_ref, s0g_ref, s1g_ref, y_ref, ua_ref, ug_ref, wab_ref, wgb_ref, wdb_ref,
         h_sc, acc_sc) = refs
        wab_ref[...] = wa_ref[...].astype(BF16)
        wgb_ref[...] = wg_ref[...].astype(BF16)
        wdb_ref[...] = wd_ref[...].astype(BF16)
        wa_ref, wg_ref, wd_ref = wab_ref, wgb_ref, wdb_ref
    t = pl.program_id(0)
    f = pl.program_id(1)
    w_sub = min(tf, 256)
    n_c = tf // w_sub

    @pl.when(f == 0)
    def _():
        x = x_ref[...]
        h_sc[...] = (x * lax.rsqrt(jnp.mean(x * x, axis=-1, keepdims=True) + EPS)
                     * gf_ref[...]).astype(BF16)
        acc_sc[...] = jnp.zeros_like(acc_sc)

    h = h_sc[...]

    if sequential:
        for which, p_ref in enumerate((pa_ref, pg_ref)):
            @pl.when(t == 0)
            def _():
                carry_sc[which, f] = p_ref[...]

            u_sc[which, 0:8, :] = carry_sc[which, f]

    def up(c):
        cs = slice(c * w_sub, (c + 1) * w_sub)
        return (jnp.dot(h, wa_ref[:, cs], preferred_element_type=F32),
                jnp.dot(h, wg_ref[:, cs], preferred_element_type=F32))

    def conv(u, which, c, cw_ref, cb_ref, u_out_ref):
        cs = slice(c * w_sub, (c + 1) * w_sub)
        if sequential:
            u_sc[which, 8:8 + tm, cs] = u
            s1 = u_sc[which, 7:7 + tm, cs]
            s0 = u_sc[which, 6:6 + tm, cs]
        else:
            s0 = (s0a_ref, s0g_ref)[which][:, cs]
            s1 = (s1a_ref, s1g_ref)[which][:, cs]
            u_out_ref[:, cs] = u
        return cb_ref[:, cs] + cw_ref[0:1, cs] * s0 + cw_ref[1:2, cs] * s1 + cw_ref[2:3, cs] * u

    pending = [up(0)]
    down = None
    for c in range(n_c):
        u_a, u_g = pending.pop(0)
        if c + 1 < n_c:
            pending.append(up(c + 1))
        ya = conv(u_a, 0, c, cwa_ref, cba_ref, ua_ref)
        yg = conv(u_g, 1, c, cwg_ref, cbg_ref, ug_ref)
        act = (ya * jax.nn.sigmoid(ya) * yg).astype(BF16)
        part = jnp.dot(act, wd_ref[c * w_sub:(c + 1) * w_sub, :], preferred_element_type=F32)
        down = part if down is None else down + part
    acc_sc[...] += down

    if sequential:
        for which, u_out_ref in enumerate((ua_ref, ug_ref)):
            tail8 = u_sc[which, tm:tm + 8, :]
            carry_sc[which, f] = tail8
            u_out_ref[...] = tail8

    @pl.when(f == n_f - 1)
    def _():
        x2 = x_ref[...] + acc_sc[...]
        y_ref[...] = x2 * lax.rsqrt(jnp.mean(x2 * x2, axis=-1, keepdims=True) + EPS) * gl_ref[...]


def _ffn(x, g_ffn, g_final, w_a, w_g, w_down, conv_w, conv_b, *, prefix=None, s0=None, s1=None):
    m, d = x.shape
    d_ff = w_down.shape[0]
    sequential = prefix is not None
    tm = _pick(m, 512)
    tf = _pick(d_ff, 512)
    n_f = d_ff // tf
    n_t = m // tm
    g_off = 0 if sequential else n_f
    col_a = lambda r, w: pl.BlockSpec((r, w), lambda t, f: (0, f))
    col_g = lambda r, w: pl.BlockSpec((r, w), lambda t, f: (0, f + n_f))
    wd_spec = pl.BlockSpec((tf, d), lambda t, f: (f, 0))
    in_specs = [pl.BlockSpec((tm, d), lambda t, f: (t, 0)),
                pl.BlockSpec((1, d), lambda t, f: (0, 0)),
                pl.BlockSpec((1, d), lambda t, f: (0, 0)),
                col_a(d, tf), pl.BlockSpec((d, tf), lambda t, f: (0, f + g_off)), wd_spec,
                col_a(CONV_W, tf), col_g(CONV_W, tf), col_a(1, tf), col_g(1, tf)]
    args = [x, g_ffn, g_final, w_a, w_g, w_down, conv_w, conv_w, conv_b, conv_b]
    scratch = [pltpu.VMEM((tm, d), BF16), pltpu.VMEM((tm, d), F32)]
    out_specs = [pl.BlockSpec((tm, d), lambda t, f: (t, 0))]
    out_shape = [jax.ShapeDtypeStruct((m, d), F32)]
    if sequential:
        in_specs += [col_a(8, tf), col_g(8, tf)]
        args += [prefix, prefix]
        u_rows = 8 * n_t
        u_spec = pl.BlockSpec((8, tf), lambda t, f: (t, f))
        scratch += [pltpu.VMEM((2, n_f, 8, tf), F32), pltpu.VMEM((2, tm + 8, tf), F32)]
    else:
        assert n_t == 1, "the bf16 weight outputs are written once per weight block"
        rows_a = pl.BlockSpec((tm, tf), lambda t, f: (t, f))
        rows_g = pl.BlockSpec((tm, tf), lambda t, f: (t, f + n_f))
        in_specs += [rows_a, rows_a, rows_g, rows_g]
        args += [s0, s1, s0, s1]
        u_rows = m
        u_spec = rows_a
    out_specs += [u_spec, u_spec]
    out_shape += [jax.ShapeDtypeStruct((u_rows, d_ff), F32)] * 2
    if not sequential:
        out_specs += [col_a(d, tf), col_a(d, tf), wd_spec]
        out_shape += [jax.ShapeDtypeStruct((d, d_ff), BF16)] * 2 + [jax.ShapeDtypeStruct((d_ff, d), BF16)]
    y, ua, ug, *w_bf16 = pl.pallas_call(
        functools.partial(_ffn_kernel, sequential=sequential, tm=tm, tf=tf, n_f=n_f),
        grid=(n_t, n_f),
        in_specs=in_specs,
        out_specs=out_specs,
        out_shape=out_shape,
        scratch_shapes=scratch,
        compiler_params=_cparams(2),
        name="conv_ffn_seq" if sequential else "conv_ffn_rows",
    )(*args)
    return y, jnp.concatenate([ua, ug], axis=1), w_bf16 or None


def kernel(x_prompt, x_sample, cache_k, cache_v, page_table, state_C, state_n, state_m, state_conv,
           meta_tokens, g_mix, w_in, b_i, b_f, lam_q1, lam_k1, lam_q2, lam_k2, g_da, g_ml, w_out,
           g_ffn, w_up, conv_w, conv_b, w_down, g_final):
    depth = w_in.shape[0]
    assert depth == 1, "single-layer trunk"
    assert x_prompt.shape[0] == 1 and x_sample.shape[1] == 1
    seq, d = x_prompt.shape[1:]
    n_dec = x_sample.shape[0]
    assert N_META + n_dec <= TAIL_ROWS
    d_ff = w_down.shape[1]
    n_pool = cache_k.shape[1]
    gate_off = 7 * GROUP_W
    s_lo, s_hi = N_META, N_META + n_dec

    x_main = x_prompt[0]
    x_tail = jnp.concatenate([meta_tokens.astype(F32), x_sample[:, 0],
                              jnp.zeros((TAIL_ROWS - s_hi, d), F32)], axis=0)
    w_in0 = jnp.swapaxes(w_in[0], 0, 1)
    wg = jnp.zeros((2 * HEAD_W, d), F32)
    wg = wg.at[:N_HEADS].set(w_in0[gate_off:gate_off + N_HEADS])
    wg = wg.at[HEAD_W:HEAD_W + N_HEADS].set(w_in0[gate_off + N_HEADS:gate_off + 2 * N_HEADS])
    wg = wg.astype(BF16)
    gbias = jnp.zeros((1, 2 * HEAD_W), F32)
    gbias = gbias.at[0, :N_HEADS].set(b_i[0]).at[0, HEAD_W:HEAD_W + N_HEADS].set(b_f[0])
    lam4 = jnp.stack([lam_q1[0], lam_k1[0], lam_q2[0], lam_k2[0]]).astype(F32)
    g_da2 = g_da[0].reshape(1, HEAD_W)
    g_ml2 = g_ml[0].reshape(1, GROUP_W)
    g_mix2 = g_mix[0].reshape(1, d)
    g_ffn2 = g_ffn[0].reshape(1, d)
    g_fin2 = g_final.reshape(1, d)
    conv_w2 = conv_w[0]
    conv_b2 = conv_b[0].reshape(1, 2 * d_ff)
    k_scale = HEAD_W ** -0.5
    q_scale = DA_QK ** -0.5 * LOG2E

    other_groups = ((0, 3, 4, 5, 6), (q_scale, 1.0, k_scale, 1.0, 1.0))

    h_t, ig_t, lf_t = _pre(x_tail, g_mix2, wg, gbias)
    kv_f_t, kv_b_t = _proj(h_t, w_in0, (1, 2), (1.0, 1.0), True)
    oth_t = _proj(h_t, w_in0, *other_groups, False)
    pt = dict(kv_b=kv_b_t, oth=oth_t, ig=ig_t, lf=lf_t, k_f=kv_f_t[0], v_f=kv_f_t[1],
              dq=oth_t[0], mq=oth_t[1], mk=oth_t[2], mv=oth_t[3], mo=oth_t[4])

    h_m, ig_m, lf_m = _pre(x_main, g_mix2, wg, gbias)
    k_rows, k_b_m = _proj_rows(h_m, w_in0, 1, pt["k_f"][:N_META])
    v_rows, v_b_m = _proj_rows(h_m, w_in0, 2, pt["v_f"][:N_META])
    pm = dict(oth=_proj(h_m, w_in0, *other_groups, False), ig=ig_m, lf=lf_m)

    tk = _pick(seq, 512)
    tq = _pick(seq, 1024)
    pairs = [(i, j) for i in range(seq // tq) for j in range((i + 1) * tq // tk)]
    kv_main = ((k_b_m[None], 0), (v_b_m[None], 0))
    kv_meta = ((pt["kv_b"], 0), (pt["kv_b"], 1))
    da_main = _attn((pm["oth"], 0), *kv_main, *kv_meta, lam4, g_da2,
                    n_q_rows=seq, tq=tq, tk=tk, q_pos0=0, prefix_off=FAR, pairs=pairs)
    da_meta = _attn((pt["oth"], 0), *kv_main, *kv_meta, lam4, g_da2,
                    n_q_rows=N_META, tq=N_META, tk=tk, q_pos0=-FAR, prefix_off=0, pairs=[(0, 0)])

    c_zero = jnp.zeros((N_HEADS, HEAD_W, 2 * HEAD_W), F32)
    m_zero = jnp.zeros((1, HEAD_W), F32)
    ml_meta, c_meta, m_meta = _mlstm(pt["oth"], pt["ig"], pt["lf"], c_zero, m_zero, g_ml2,
                                     n_rows=N_META, chunk=N_META)
    ml_main, c_fin, m_fin = _mlstm(pm["oth"], pm["ig"], pm["lf"], c_meta, m_meta, g_ml2,
                                   n_rows=seq, chunk=_pick(seq, 256))

    f32_rows = lambda a: a[s_lo:s_hi].astype(F32)
    heads = lambda a: a.reshape(n_dec, N_HEADS, HEAD_W)
    twice = lambda a: jnp.concatenate([heads(a), heads(a)], axis=1)
    ck = cache_k[0].reshape(n_pool, PAGE_SIZE * N_HEADS, HEAD_W)
    cv = cache_v[0].reshape(n_pool, PAGE_SIZE * N_HEADS, HEAD_W)
    da_smp = _decode_attn(page_table, twice(f32_rows(pt["dq"])), twice(pt["k_f"][s_lo:s_hi]),
                          twice(pt["v_f"][s_lo:s_hi]), ck, cv, lam4, g_da2)
    da_smp = da_smp.reshape(n_dec, GROUP_W).astype(BF16)
    m_pad = jnp.zeros((n_dec, 1, HEAD_W), F32).at[:, 0, :N_HEADS].set(state_m[0].astype(F32))
    ml_smp, c_smp, n_smp, m_smp = _mlstm_step(
        heads(f32_rows(pt["mq"])), heads(f32_rows(pt["mk"])), heads(f32_rows(pt["mv"])),
        heads(pt["mo"][s_lo:s_hi]), pt["ig"][s_lo:s_hi, None, :], pt["lf"][s_lo:s_hi, None, :],
        state_C[0].astype(F32), state_n[0].astype(F32), m_pad, g_ml2.reshape(N_HEADS, HEAD_W))
    ml_smp = ml_smp.reshape(n_dec, GROUP_W).astype(BF16)

    pad_rows = jnp.zeros((TAIL_ROWS - s_hi, GROUP_W), BF16)
    da_tail = jnp.concatenate([da_meta, da_smp, pad_rows], axis=0)
    ml_tail = jnp.concatenate([ml_meta, ml_smp, pad_rows], axis=0)
    x1_main = _out_proj(da_main, ml_main, x_main, w_out[0])
    x1_tail = _out_proj(da_tail, ml_tail, x_tail, w_out[0])

    hist = jnp.zeros((2, TAIL_ROWS, 2 * d_ff), F32).at[:, s_lo:s_hi].set(
        jnp.swapaxes(state_conv[0].astype(F32), 0, 1))
    y_tail, u_tail, (wa_b, wg_b, wd_b) = _ffn(x1_tail, g_ffn2, g_fin2, w_up[0], w_up[0], w_down[0],
                                              conv_w2, conv_b2, s0=hist[0], s1=hist[1])
    y_main, u_last, _ = _ffn(x1_main, g_ffn2, g_fin2, wa_b, wg_b, wd_b, conv_w2, conv_b2,
                             prefix=u_tail[N_META - 8:N_META])

    y_prompt = y_main[None]
    y_sample = y_tail[s_lo:s_hi, None, :]
    k_rows_p = k_rows[None, None]
    v_rows_p = v_rows[None, None]
    k_rows_s = pt["k_f"][s_lo:s_hi].reshape(1, n_dec, 1, N_HEADS, HEAD_W)
    v_rows_s = pt["v_f"][s_lo:s_hi].reshape(1, n_dec, 1, N_HEADS, HEAD_W)
    c_p = c_fin[None, None, :, :, :HEAD_W]
    n_p = c_fin[None, None, :, :, HEAD_W]
    m_p = m_fin[None, :, :N_HEADS]
    c_s = c_smp[None]
    n_s = n_smp[None]
    m_s = m_smp[None, :, 0, :N_HEADS]
    conv_p = u_last[None, None, -2:, :]
    conv_s = jnp.stack([state_conv[0][:, 1].astype(F32), u_tail[s_lo:s_hi]], axis=1)[None]
    return (y_prompt, y_sample, k_rows_p, v_rows_p, k_rows_s, v_rows_s,
            c_p, n_p, m_p, c_s, n_s, m_s, conv_p, conv_s)
```

```python
import functools

import jax
import jax.numpy as jnp
from jax import lax
from jax.experimental import pallas as pl
from jax.experimental.pallas import tpu as pltpu

F32 = jnp.float32
BF16 = jnp.bfloat16

N_META = 16
N_HEADS = 8
HEAD_W = 128
DA_QK = 64
PAGE_SIZE = 128
CONV_W = 3
EPS = 1e-6
GROUP_W = N_HEADS * HEAD_W
LAM_INIT = 0.8 - 0.6 * 1.0
TAIL_ROWS = 128
NEG_INF = float("-inf")
LOG2E = 1.4426950408889634
FAR = 1 << 24

V7X_VMEM_BYTES = 64 * 1024 * 1024
VMEM_LIMIT = V7X_VMEM_BYTES * 13 // 16

_NT = (((1,), (1,)), ((), ()))


def _cparams(n_axes):
    return pltpu.CompilerParams(dimension_semantics=("arbitrary",) * n_axes,
                                vmem_limit_bytes=VMEM_LIMIT)


def _pick(total, pref):
    t = min(pref, total)
    while total % t:
        t //= 2
    return t


def _pre_kernel(x_ref, g_ref, wg_ref, bias_ref, h_ref, ig_ref, lf_ref):
    x = x_ref[...]
    y = x * lax.rsqrt(jnp.mean(x * x, axis=-1, keepdims=True) + EPS) * g_ref[...]
    hb = y.astype(BF16)
    h_ref[...] = hb
    z = lax.dot_general(hb, wg_ref[...], _NT, preferred_element_type=F32) + bias_ref[...]
    ig_ref[...] = z[:, :HEAD_W]
    zf = z[:, HEAD_W:]
    lf_ref[...] = jnp.minimum(zf, 0.0) - jnp.log1p(jnp.exp(-jnp.abs(zf)))


def _pre(x, g, wg, bias):
    m, d = x.shape
    tm = _pick(m, 512)
    return pl.pallas_call(
        _pre_kernel,
        grid=(m // tm,),
        in_specs=[pl.BlockSpec((tm, d), lambda i: (i, 0)),
                  pl.BlockSpec((1, d), lambda i: (0, 0)),
                  pl.BlockSpec((2 * HEAD_W, d), lambda i: (0, 0)),
                  pl.BlockSpec((1, 2 * HEAD_W), lambda i: (0, 0))],
        out_specs=[pl.BlockSpec((tm, d), lambda i: (i, 0)),
                   pl.BlockSpec((tm, HEAD_W), lambda i: (i, 0)),
                   pl.BlockSpec((tm, HEAD_W), lambda i: (i, 0))],
        out_shape=[jax.ShapeDtypeStruct((m, d), BF16),
                   jax.ShapeDtypeStruct((m, HEAD_W), F32),
                   jax.ShapeDtypeStruct((m, HEAD_W), F32)],
        compiler_params=_cparams(1),
        name="pre_norm_gates",
    )(x, g, wg, bias)


def _proj_kernel(h_ref, w_ref, s_ref, *rest, with_f32):
    if with_f32:
        of_ref, ob_ref, wb_ref = rest
    else:
        ob_ref, wb_ref = rest

    @pl.when(pl.program_id(1) == 0)
    def _():
        wb_ref[...] = w_ref[...].astype(BF16)

    z = lax.dot_general(h_ref[...], wb_ref[...], _NT, preferred_element_type=F32)
    if with_f32:
        for hd in range(N_HEADS):
            of_ref[0, :, hd, :] = z[:, hd * HEAD_W:(hd + 1) * HEAD_W]
    ob_ref[0] = (z * s_ref[0, :, 0:1]).astype(BF16)


def _proj(h, w_t, groups, scales, with_f32):
    m, d = h.shape
    tm = _pick(m, 1024)
    ng = len(groups)
    first, rest_off = groups[0], groups[1] - 1 if ng > 1 else 0
    assert list(groups) == [first] + [rest_off + k for k in range(1, ng)]
    s_arr = jnp.broadcast_to(jnp.asarray(scales, F32)[:, None, None], (ng, 1, HEAD_W))

    def w_map(j, i):
        return (jnp.where(j == 0, first, j + rest_off), 0)

    out_specs = [pl.BlockSpec((1, tm, GROUP_W), lambda j, i: (j, i, 0))]
    out_shape = [jax.ShapeDtypeStruct((ng, m, GROUP_W), BF16)]
    if with_f32:
        out_specs = [pl.BlockSpec((1, tm, N_HEADS, HEAD_W), lambda j, i: (j, i, 0, 0))] + out_specs
        out_shape = [jax.ShapeDtypeStruct((ng, m, N_HEADS, HEAD_W), F32)] + out_shape
    outs = pl.pallas_call(
        functools.partial(_proj_kernel, with_f32=with_f32),
        grid=(ng, m // tm),
        in_specs=[pl.BlockSpec((tm, d), lambda j, i: (i, 0)),
                  pl.BlockSpec((GROUP_W, d), w_map),
                  pl.BlockSpec((1, 1, HEAD_W), lambda j, i: (j, 0, 0))],
        out_specs=out_specs,
        out_shape=out_shape,
        scratch_shapes=[pltpu.VMEM((GROUP_W, d), BF16)],
        compiler_params=_cparams(2),
        name="in_proj_f32" if with_f32 else "in_proj",
    )(h, w_t, s_arr)
    return outs if with_f32 else outs[0]


def _proj_rows_kernel(h_ref, w_ref, meta_ref, rows_out, ob_ref, wb_ref, stage, sems, meta_sem,
                      *, tm, n_i, row0):
    i = pl.program_id(0)
    slot = i % 2

    def row_copy(sl, blk):
        return pltpu.make_async_copy(stage.at[sl], rows_out.at[pl.ds(row0 + blk * tm, tm)], sems.at[sl])

    @pl.when(i == 0)
    def _():
        wb_ref[...] = w_ref[...].astype(BF16)
        meta_copy = pltpu.make_async_copy(meta_ref, rows_out.at[pl.ds(0, row0)], meta_sem.at[0])
        meta_copy.start()
        meta_copy.wait()

    @pl.when(i >= 2)
    def _():
        row_copy(slot, i - 2).wait()

    z = lax.dot_general(h_ref[...], wb_ref[...], _NT, preferred_element_type=F32)
    for hd in range(N_HEADS):
        stage[slot, :, hd, :] = z[:, hd * HEAD_W:(hd + 1) * HEAD_W]
    ob_ref[...] = z.astype(BF16)
    row_copy(slot, i).start()

    @pl.when(i == n_i - 1)
    def _():
        row_copy(slot, i).wait()
        if n_i >= 2:
            row_copy(1 - slot, i - 1).wait()


def _proj_rows(h, w_t, group, meta_rows):
    m, d = h.shape
    row0 = meta_rows.shape[0]
    tm = _pick(m, 1024)
    n_i = m // tm
    return pl.pallas_call(
        functools.partial(_proj_rows_kernel, tm=tm, n_i=n_i, row0=row0),
        grid=(n_i,),
        in_specs=[pl.BlockSpec((tm, d), lambda i: (i, 0)),
                  pl.BlockSpec((GROUP_W, d), lambda i: (group, 0)),
                  pl.BlockSpec((row0, N_HEADS, HEAD_W), lambda i: (0, 0, 0))],
        out_specs=[pl.BlockSpec(memory_space=pl.ANY),
                   pl.BlockSpec((tm, GROUP_W), lambda i: (i, 0))],
        out_shape=[jax.ShapeDtypeStruct((row0 + m, N_HEADS, HEAD_W), F32),
                   jax.ShapeDtypeStruct((m, GROUP_W), BF16)],
        scratch_shapes=[pltpu.VMEM((GROUP_W, d), BF16),
                        pltpu.VMEM((2, tm, N_HEADS, HEAD_W), F32),
                        pltpu.SemaphoreType.DMA((2,)),
                        pltpu.SemaphoreType.DMA((1,))],
        compiler_params=_cparams(1),
        name="in_proj_rows",
    )(h, w_t, meta_rows)


def _attn_kernel(qi_ref, kj_ref, q_ref, k_ref, v_ref, kp_ref, vp_ref, lam_ref, g_ref,
                 o_ref, m_sc, l_sc, acc_sc, *, tq, tk, sub, ahead, q_pos0, prefix_off, n_steps,
                 mask_deltas, hps):
    step = pl.program_id(1)
    i = qi_ref[step]
    j = kj_ref[step]
    lane = lax.broadcasted_iota(jnp.int32, (tq, HEAD_W), 1)
    hsl = [slice(hd * HEAD_W, (hd + 1) * HEAD_W) for hd in range(hps)]
    q2 = []
    for hs in hsl:
        q = q_ref[:, hs]
        zero = jnp.zeros_like(q)
        q2.append(jnp.concatenate([jnp.where(lane < DA_QK, q, zero), jnp.where(lane >= DA_QK, q, zero)],
                                  axis=0))
    n_sub = 2 * tq // sub

    @pl.when(j == 0)
    def _():
        for hd, hs in enumerate(hsl):
            own = slice(hd * 2 * tq, (hd + 1) * 2 * tq)
            sp = lax.dot_general(q2[hd], kp_ref[:, hs], _NT, preferred_element_type=F32)
            col = lax.broadcasted_iota(jnp.int32, sp.shape, 1)
            r = lax.broadcasted_iota(jnp.int32, sp.shape, 0)
            ok = (col < N_META) & (col <= jnp.where(r >= tq, r - tq, r) + prefix_off)
            sp = jnp.where(ok, sp, NEG_INF)
            m0 = jnp.max(sp, axis=1, keepdims=True)
            p = jnp.exp2(sp - m0)
            m_sc[own] = jnp.broadcast_to(m0, (2 * tq, HEAD_W))
            l_sc[own] = jnp.broadcast_to(jnp.sum(p, axis=1, keepdims=True), (2 * tq, HEAD_W))
            acc_sc[own] = jnp.dot(p.astype(BF16), vp_ref[:, hs], preferred_element_type=F32)

    def block(delta):
        def first_visible(r):
            return FAR if delta is None else delta + (r * sub) % tq
        live = [(hd, r) for hd in range(hps) for r in range(n_sub) if first_visible(r) + sub - 1 >= 0]
        if not live:
            return
        k = [k_ref[:, hs] for hs in hsl]
        ones = jnp.ones((tk, HEAD_W), BF16)
        v_aug = [jnp.concatenate([v_ref[:, hs], ones], axis=1) for hs in hsl]
        rel = (lax.broadcasted_iota(jnp.int32, (sub, tk), 1)
               - lax.broadcasted_iota(jnp.int32, (sub, tk), 0))
        m_all, l_all, acc_all = m_sc[...], l_sc[...], acc_sc[...]
        new = []

        def scores(unit):
            hd, r = unit
            return lax.dot_general(q2[hd][r * sub:(r + 1) * sub], k[hd], _NT, preferred_element_type=F32)

        def state_rows(unit):
            hd, r = unit
            return slice(hd * 2 * tq + r * sub, hd * 2 * tq + (r + 1) * sub)

        pending = [scores(u) for u in live[:ahead]]
        for n, (hd, r) in enumerate(live):
            rows = state_rows((hd, r))
            s = pending.pop(0)
            if n + ahead < len(live):
                pending.append(scores(live[n + ahead]))
            if first_visible(r) < tk - 1:
                s = jnp.where(rel <= first_visible(r), s, NEG_INF)
            m_prev = m_all[rows]
            m_new = jnp.maximum(m_prev, jnp.max(s, axis=1, keepdims=True))
            alpha = jnp.exp2(m_prev - m_new)
            p = jnp.exp2(s - jnp.concatenate([m_new] * (tk // HEAD_W), axis=1))
            pv = jnp.dot(p.astype(BF16), v_aug[hd], preferred_element_type=F32)
            new.append((m_new, alpha * l_all[rows] + pv[:, HEAD_W:], alpha * acc_all[rows] + pv[:, :HEAD_W]))
        for unit, (m_new, l_new, acc_new) in zip(live, new):
            rows = state_rows(unit)
            m_sc[rows] = m_new
            l_sc[rows] = l_new
            acc_sc[rows] = acc_new

    delta_now = q_pos0 + i * tq - j * tk
    pl.when(delta_now >= tk - 1)(lambda: block(None))
    for delta in mask_deltas:
        pl.when(delta_now == delta)(functools.partial(block, delta))

    is_last = jnp.logical_or(step == n_steps - 1, qi_ref[step + 1] != i)

    @pl.when(is_last)
    def _():
        o = acc_sc[...] / l_sc[...]
        lam = (jnp.exp(jnp.sum(lam_ref[0:1, :] * lam_ref[1:2, :], axis=1, keepdims=True))
               - jnp.exp(jnp.sum(lam_ref[2:3, :] * lam_ref[3:4, :], axis=1, keepdims=True)) + LAM_INIT)
        for hd, hs in enumerate(hsl):
            base = hd * 2 * tq
            d = o[base:base + tq] - lam * o[base + tq:base + 2 * tq]
            y = d * lax.rsqrt(jnp.mean(d * d, axis=-1, keepdims=True) + EPS) * g_ref[...]
            o_ref[:, hs] = (y * (1.0 - LAM_INIT)).astype(o_ref.dtype)


def _attn(q, k, v, kp, vp, lam4, g_da, *, n_q_rows, tq, tk, q_pos0, prefix_off, pairs, hps=8):
    qi = jnp.asarray([p[0] for p in pairs] + [-1], jnp.int32)
    kj = jnp.asarray([p[1] for p in pairs] + [0], jnp.int32)
    n_steps = len(pairs)
    deltas = {q_pos0 + pi * tq - pj * tk for pi, pj in pairs}
    kern = functools.partial(_attn_kernel, tq=tq, tk=tk, sub=min(256, 2 * tq), ahead=3, q_pos0=q_pos0,
                             prefix_off=prefix_off, n_steps=n_steps,
                             mask_deltas=tuple(sorted(dl for dl in deltas if dl < tk - 1)), hps=hps)
    hw = hps * HEAD_W
    grid_spec = pltpu.PrefetchScalarGridSpec(
        num_scalar_prefetch=2,
        grid=(N_HEADS // hps, n_steps),
        in_specs=[pl.BlockSpec((None, tq, hw), lambda h, s, qi, kj: (q[1], qi[s], h)),
                  pl.BlockSpec((None, tk, hw), lambda h, s, qi, kj: (k[1], kj[s], h)),
                  pl.BlockSpec((None, tk, hw), lambda h, s, qi, kj: (v[1], kj[s], h)),
                  pl.BlockSpec((None, TAIL_ROWS, hw), lambda h, s, qi, kj: (kp[1], 0, h)),
                  pl.BlockSpec((None, TAIL_ROWS, hw), lambda h, s, qi, kj: (vp[1], 0, h)),
                  pl.BlockSpec((4, DA_QK), lambda h, s, qi, kj: (0, 0)),
                  pl.BlockSpec((1, HEAD_W), lambda h, s, qi, kj: (0, 0))],
        out_specs=pl.BlockSpec((tq, hw), lambda h, s, qi, kj: (qi[s], h)),
        scratch_shapes=[pltpu.VMEM((hps * 2 * tq, HEAD_W), F32)] * 3)
    return pl.pallas_call(
        kern, grid_spec=grid_spec,
        out_shape=jax.ShapeDtypeStruct((n_q_rows, GROUP_W), BF16),
        compiler_params=_cparams(2),
        name="diff_attn_prompt",
    )(qi, kj, q[0], k[0], v[0], kp[0], vp[0], lam4, g_da)


def _mlstm_kernel(q_ref, k_ref, v_ref, o_ref, ig_ref, lf_ref, c0_ref, m0_ref, g_ref,
                  y_ref, c_out_ref, m_out_ref, c_sc, m_sc, *, chunk, n_chunks):
    c = pl.program_id(0)

    @pl.when(c == 0)
    def _():
        c_sc[...] = c0_ref[...]
        m_sc[...] = m0_ref[...]

    L = chunk
    ig = ig_ref[...]
    lf = lf_ref[...]
    row = lax.broadcasted_iota(jnp.int32, (L, L), 0)
    col = lax.broadcasted_iota(jnp.int32, (L, L), 1)
    causal = col <= row
    b_col = jnp.dot(causal.astype(F32), lf, preferred_element_type=F32,
                    precision=lax.Precision.HIGHEST)
    b_row = b_col.T
    ig_row = ig.T
    m_prev_all = m_sc[...]
    lane1 = lax.broadcasted_iota(jnp.int32, (1, HEAD_W), 1)
    m_next_all = m_prev_all
    ones = jnp.ones((L, HEAD_W), BF16)

    heads = range(N_HEADS)
    hsl = [slice(h * HEAD_W, (h + 1) * HEAD_W) for h in heads]
    c_old = [c_sc[h] for h in heads]
    q_l = [q_ref[:, hs] for hs in hsl]
    k_l = [k_ref[:, hs] for hs in hsl]
    v_aug = [jnp.concatenate([v_ref[:, hs], ones], axis=1) for hs in hsl]
    s_l = [lax.dot_general(q_l[h], k_l[h], _NT, preferred_element_type=F32) for h in heads]
    qc_l = [jnp.dot(q_l[h], c_old[h].astype(BF16), preferred_element_type=F32) for h in heads]

    gate = []
    for h in heads:
        bc = b_col[:, h:h + 1]
        m_prev = m_prev_all[:, h:h + 1]
        dm = jnp.where(causal, ig_row[h:h + 1, :] - b_row[h:h + 1, :], NEG_INF)
        rel_m = jnp.maximum(m_prev, jnp.max(dm, axis=1, keepdims=True))
        m_t = bc + rel_m
        a = (s_l[h] * jnp.exp(dm - rel_m)).astype(BF16)
        gate.append((bc, m_prev, m_t, jnp.exp(m_prev - rel_m), a))

    num_l = [jnp.dot(gate[h][4], v_aug[h], preferred_element_type=F32) + gate[h][3] * qc_l[h]
             for h in heads]

    kw_l = []
    for h in heads:
        bc, m_prev, m_t, _, _ = gate[h]
        b_last = bc[L - 1:L]
        m_new = m_t[L - 1:L]
        ws = jnp.exp(b_last - bc + ig[:, h:h + 1] - m_new)
        kw_l.append(((k_l[h].astype(F32) * ws).T.astype(BF16),
                     jnp.exp(b_last + m_prev - m_new)))
        m_next_all = jnp.where(lane1 == h, m_new, m_next_all)

    for h in heads:
        kw_t, decay = kw_l[h]
        c_sc[h] = decay * c_old[h] + jnp.dot(kw_t, v_aug[h], preferred_element_type=F32)

    for h in heads:
        hs = hsl[h]
        num = num_l[h][:, :HEAD_W]
        den = num_l[h][:, HEAD_W:]
        hh = num / jnp.maximum(jnp.abs(den), jnp.exp(-gate[h][2]))
        y = hh * lax.rsqrt(jnp.mean(hh * hh, axis=-1, keepdims=True) + EPS) * g_ref[:, hs]
        y_ref[:, hs] = (y * jax.nn.sigmoid(o_ref[:, hs].astype(F32))).astype(y_ref.dtype)

    m_sc[...] = m_next_all

    @pl.when(c == n_chunks - 1)
    def _():
        c_out_ref[...] = c_sc[...]
        m_out_ref[...] = m_sc[...]


def _mlstm(proj, ig, lf, c0, m0, g_ml, *, n_rows, chunk):
    n_chunks = n_rows // chunk
    row_blk = lambda w: pl.BlockSpec((chunk, w), lambda c: (c, 0))
    grp_blk = lambda g: pl.BlockSpec((None, chunk, GROUP_W), lambda c: (g, c, 0))
    whole = lambda shape: pl.BlockSpec(shape, lambda c: (0,) * len(shape))
    return pl.pallas_call(
        functools.partial(_mlstm_kernel, chunk=chunk, n_chunks=n_chunks),
        grid=(n_chunks,),
        in_specs=[grp_blk(1), grp_blk(2), grp_blk(3), grp_blk(4),
                  row_blk(HEAD_W), row_blk(HEAD_W),
                  whole((N_HEADS, HEAD_W, 2 * HEAD_W)), whole((1, HEAD_W)), whole((1, GROUP_W))],
        out_specs=[row_blk(GROUP_W), whole((N_HEADS, HEAD_W, 2 * HEAD_W)), whole((1, HEAD_W))],
        out_shape=[jax.ShapeDtypeStruct((n_rows, GROUP_W), BF16),
                   jax.ShapeDtypeStruct((N_HEADS, HEAD_W, 2 * HEAD_W), F32),
                   jax.ShapeDtypeStruct((1, HEAD_W), F32)],
        scratch_shapes=[pltpu.VMEM((N_HEADS, HEAD_W, 2 * HEAD_W), F32), pltpu.VMEM((1, HEAD_W), F32)],
        compiler_params=_cparams(1),
        name="mlstm_prompt",
    )(proj, proj, proj, proj, ig, lf, c0, m0, g_ml)


def _decode_kernel(pt_ref, q_ref, kn_ref, vn_ref, lam_ref, g_ref, *rest, pages_per_step, n_steps):
    k_refs = rest[:pages_per_step]
    v_refs = rest[pages_per_step:2 * pages_per_step]
    o_ref, m_sc, l_sc, acc_sc = rest[2 * pages_per_step:]
    j = pl.program_id(1)
    n_rows = 2 * N_HEADS
    page_rows = PAGE_SIZE * N_HEADS
    q16 = q_ref[0]
    rr = lax.broadcasted_iota(jnp.int32, (n_rows, HEAD_W), 0)
    cc = lax.broadcasted_iota(jnp.int32, (n_rows, HEAD_W), 1)
    qm = jnp.where(cc // DA_QK == rr // N_HEADS, q16, 0.0)

    @pl.when(j == 0)
    def _():
        m_sc[...] = jnp.broadcast_to(jnp.sum(qm * kn_ref[0], axis=1, keepdims=True), m_sc.shape)
        l_sc[...] = jnp.ones_like(l_sc)
        acc_sc[...] = vn_ref[0]

    qb = qm.astype(BF16)
    key_head = lax.broadcasted_iota(jnp.int32, (n_rows, page_rows), 1) % N_HEADS
    row_head = lax.broadcasted_iota(jnp.int32, (n_rows, page_rows), 0) % N_HEADS
    own = key_head == row_head
    s_pages = [jnp.where(own, lax.dot_general(qb, kr[0].astype(BF16), _NT, preferred_element_type=F32),
                         NEG_INF) for kr in k_refs]
    s_max = s_pages[0]
    for s in s_pages[1:]:
        s_max = jnp.maximum(s_max, s)
    m_prev = m_sc[...]
    m_new = jnp.maximum(m_prev, jnp.max(s_max, axis=1, keepdims=True))
    alpha = jnp.exp2(m_prev - m_new)
    m_rep = jnp.concatenate([m_new] * (page_rows // HEAD_W), axis=1)
    l_add = jnp.zeros((n_rows, 1), F32)
    pv = jnp.zeros((n_rows, HEAD_W), F32)
    for s, vr in zip(s_pages, v_refs):
        p = jnp.exp2(s - m_rep)
        l_add += jnp.sum(p, axis=1, keepdims=True)
        pv += jnp.dot(p.astype(BF16), vr[0].astype(BF16), preferred_element_type=F32)
    l_sc[...] = alpha * l_sc[...] + l_add
    acc_sc[...] = alpha * acc_sc[...] + pv
    m_sc[...] = m_new

    @pl.when(j == n_steps - 1)
    def _():
        o = acc_sc[...] / l_sc[...]
        lam = (jnp.exp(jnp.sum(lam_ref[0:1, :] * lam_ref[1:2, :], axis=1, keepdims=True))
               - jnp.exp(jnp.sum(lam_ref[2:3, :] * lam_ref[3:4, :], axis=1, keepdims=True)) + LAM_INIT)
        d = o[:N_HEADS] - lam * o[N_HEADS:]
        y = d * lax.rsqrt(jnp.mean(d * d, axis=-1, keepdims=True) + EPS) * g_ref[...]
        o_ref[0] = y * (1.0 - LAM_INIT)


def _decode_attn(page_table, q, k_new, v_new, cache_k, cache_v, lam4, g_da):
    b, n_pages = page_table.shape
    pps = _pick(n_pages, 16)
    n_steps = n_pages // pps
    pt_flat = page_table.reshape(-1)
    n_rows = 2 * N_HEADS

    def page_spec(g):
        return pl.BlockSpec((1, PAGE_SIZE * N_HEADS, HEAD_W),
                            lambda bi, j, pt: (pt[bi * n_pages + j * pps + g], 0, 0))

    seq_spec = pl.BlockSpec((1, n_rows, HEAD_W), lambda bi, j, pt: (bi, 0, 0))
    grid_spec = pltpu.PrefetchScalarGridSpec(
        num_scalar_prefetch=1,
        grid=(b, n_steps),
        in_specs=[seq_spec, seq_spec, seq_spec,
                  pl.BlockSpec((4, DA_QK), lambda bi, j, pt: (0, 0)),
                  pl.BlockSpec((1, HEAD_W), lambda bi, j, pt: (0, 0))]
                 + [page_spec(g) for g in range(pps)] + [page_spec(g) for g in range(pps)],
        out_specs=pl.BlockSpec((1, N_HEADS, HEAD_W), lambda bi, j, pt: (bi, 0, 0)),
        scratch_shapes=[pltpu.VMEM((n_rows, HEAD_W), F32), pltpu.VMEM((n_rows, HEAD_W), F32),
                        pltpu.VMEM((n_rows, HEAD_W), F32)])
    return pl.pallas_call(
        functools.partial(_decode_kernel, pages_per_step=pps, n_steps=n_steps),
        grid_spec=grid_spec,
        out_shape=jax.ShapeDtypeStruct((b, N_HEADS, HEAD_W), F32),
        compiler_params=_cparams(2),
        name="diff_attn_decode",
    )(pt_flat, q, k_new, v_new, lam4, g_da, *([cache_k] * pps), *([cache_v] * pps))


def _mlstm_step_kernel(q_ref, k_ref, v_ref, o_ref, ig_ref, lf_ref, c_ref, n_ref, m_ref, g_ref,
                       y_ref, c_out_ref, n_out_ref, m_out_ref):
    q8 = q_ref[0]
    k8 = k_ref[0]
    v8 = v_ref[0]
    n8 = n_ref[0]
    ig = ig_ref[0]
    lf = lf_ref[0]
    m = m_ref[0]
    inter = lf + m
    m_t = jnp.maximum(inter, ig)
    w_all = jnp.exp(ig - m_t)
    g_all = jnp.exp(inter - m_t)
    floor_all = jnp.exp(-m_t)
    m_out_ref[0] = m_t
    q_t = q8.T
    k_t = k8.T
    qk = jnp.sum(q8 * k8, axis=1, keepdims=True)
    qn = jnp.sum(q8 * n8, axis=1, keepdims=True)
    for h in range(N_HEADS):
        w = w_all[:, h:h + 1]
        g = g_all[:, h:h + 1]
        c_h = c_ref[0, h]
        vh = v8[h:h + 1]
        a = qk[h:h + 1] * w
        q_c = jnp.sum(q_t[:, h:h + 1] * c_h, axis=0, keepdims=True)
        num = a * vh + g * q_c
        den = a + g * qn[h:h + 1]
        hh = num / jnp.maximum(jnp.abs(den), floor_all[:, h:h + 1])
        y = hh * lax.rsqrt(jnp.mean(hh * hh, axis=-1, keepdims=True) + EPS) * g_ref[h:h + 1]
        y_ref[0, h:h + 1] = (y * jax.nn.sigmoid(o_ref[0, h:h + 1].astype(F32))).astype(y_ref.dtype)
        c_out_ref[0, h] = g * c_h + w * (k_t[:, h:h + 1] * vh)
        n_out_ref[0, h:h + 1] = g * n8[h:h + 1] + w * k8[h:h + 1]


def _mlstm_step(q, k, v, o, ig, lf, c, n, m, g_ml):
    b = q.shape[0]
    hv = pl.BlockSpec((1, N_HEADS, HEAD_W), lambda i: (i, 0, 0))
    gate = pl.BlockSpec((1, 1, HEAD_W), lambda i: (i, 0, 0))
    cs = pl.BlockSpec((1, N_HEADS, HEAD_W, HEAD_W), lambda i: (i, 0, 0, 0))
    return pl.pallas_call(
        _mlstm_step_kernel,
        grid=(b,),
        in_specs=[hv, hv, hv, hv, gate, gate, cs, hv, gate,
                  pl.BlockSpec((N_HEADS, HEAD_W), lambda i: (0, 0))],
        out_specs=[hv, cs, hv, gate],
        out_shape=[jax.ShapeDtypeStruct((b, N_HEADS, HEAD_W), F32),
                   jax.ShapeDtypeStruct(c.shape, F32),
                   jax.ShapeDtypeStruct((b, N_HEADS, HEAD_W), F32),
                   jax.ShapeDtypeStruct((b, 1, HEAD_W), F32)],
        compiler_params=_cparams(1),
        name="mlstm_sample",
    )(q, k, v, o, ig, lf, c, n, m, g_ml)


def _out_proj_kernel(da_ref, ml_ref, x_ref, w_ref, o_ref, wb_ref):
    @pl.when(pl.program_id(1) == 0)
    def _():
        wb_ref[...] = w_ref[...].astype(BF16)

    y = jnp.concatenate([da_ref[...], ml_ref[...]], axis=1)
    o_ref[...] = x_ref[...] + jnp.dot(y, wb_ref[...], preferred_element_type=F32)


def _out_proj(da, ml, x, w_out):
    m, d = x.shape
    tm = _pick(m, 1024)
    tn = _pick(d, 1024)
    return pl.pallas_call(
        _out_proj_kernel,
        grid=(d // tn, m // tm),
        in_specs=[pl.BlockSpec((tm, GROUP_W), lambda j, i: (i, 0)),
                  pl.BlockSpec((tm, GROUP_W), lambda j, i: (i, 0)),
                  pl.BlockSpec((tm, tn), lambda j, i: (i, j)),
                  pl.BlockSpec((2 * GROUP_W, tn), lambda j, i: (0, j))],
        out_specs=pl.BlockSpec((tm, tn), lambda j, i: (i, j)),
        out_shape=jax.ShapeDtypeStruct((m, d), F32),
        scratch_shapes=[pltpu.VMEM((2 * GROUP_W, tn), BF16)],
        compiler_params=_cparams(2),
        name="out_proj",
    )(da, ml, x, w_out)


def _ffn_kernel(*refs, sequential, tm, tf, n_f):
    if sequential:
        (x_ref, gf_ref, gl_ref, wa_ref, wg_ref, wd_ref, cwa_ref, cwg_ref, cba_ref, cbg_ref,
         pa_ref, pg_ref, y_ref, ua_ref, ug_ref, h_sc, acc_sc, carry_sc, u_sc) = refs
    else:
        (x_ref, gf_ref, gl_ref, wa_ref, wg_ref, wd_ref, cwa_ref, cwg_ref, cba_ref, cbg_ref,
         s0a_ref, s1a_ref, s0g_ref, s1g_ref, y_ref, ua_ref, ug_ref, wab_ref, wgb_ref, wdb_ref,
         h_sc, acc_sc) = refs
        wab_ref[...] = wa_ref[...].astype(BF16)
        wgb_ref[...] = wg_ref[...].astype(BF16)
        wdb_ref[...] = wd_ref[...].astype(BF16)
        wa_ref, wg_ref, wd_ref = wab_ref, wgb_ref, wdb_ref
    t = pl.program_id(0)
    f = pl.program_id(1)
    w_sub = min(tf, 256)
    n_c = tf // w_sub

    @pl.when(f == 0)
    def _():
        x = x_ref[...]
        h_sc[...] = (x * lax.rsqrt(jnp.mean(x * x, axis=-1, keepdims=True) + EPS)
                     * gf_ref[...]).astype(BF16)
        acc_sc[...] = jnp.zeros_like(acc_sc)

    h = h_sc[...]

    if sequential:
        for which, p_ref in enumerate((pa_ref, pg_ref)):
            @pl.when(t == 0)
            def _():
                carry_sc[which, f] = p_ref[...]

            u_sc[which, 0:8, :] = carry_sc[which, f]

    def up(c):
        cs = slice(c * w_sub, (c + 1) * w_sub)
        return (jnp.dot(h, wa_ref[:, cs], preferred_element_type=F32),
                jnp.dot(h, wg_ref[:, cs], preferred_element_type=F32))

    def conv(u, which, c, cw_ref, cb_ref, u_out_ref):
        cs = slice(c * w_sub, (c + 1) * w_sub)
        if sequential:
            u_sc[which, 8:8 + tm, cs] = u
            s1 = u_sc[which, 7:7 + tm, cs]
            s0 = u_sc[which, 6:6 + tm, cs]
        else:
            s0 = (s0a_ref, s0g_ref)[which][:, cs]
            s1 = (s1a_ref, s1g_ref)[which][:, cs]
            u_out_ref[:, cs] = u
        return cb_ref[:, cs] + cw_ref[0:1, cs] * s0 + cw_ref[1:2, cs] * s1 + cw_ref[2:3, cs] * u

    pending = [up(0)]
    down = None
    for c in range(n_c):
        u_a, u_g = pending.pop(0)
        if c + 1 < n_c:
            pending.append(up(c + 1))
        ya = conv(u_a, 0, c, cwa_ref, cba_ref, ua_ref)
        yg = conv(u_g, 1, c, cwg_ref, cbg_ref, ug_ref)
        act = (ya * jax.nn.sigmoid(ya) * yg).astype(BF16)
        part = jnp.dot(act, wd_ref[c * w_sub:(c + 1) * w_sub, :], preferred_element_type=F32)
        down = part if down is None else down + part
    acc_sc[...] += down

    if sequential:
        for which, u_out_ref in enumerate((ua_ref, ug_ref)):
            tail8 = u_sc[which, tm:tm + 8, :]
            carry_sc[which, f] = tail8
            u_out_ref[...] = tail8

    @pl.when(f == n_f - 1)
    def _():
        x2 = x_ref[...] + acc_sc[...]
        y_ref[...] = x2 * lax.rsqrt(jnp.mean(x2 * x2, axis=-1, keepdims=True) + EPS) * gl_ref[...]


def _ffn(x, g_ffn, g_final, w_a, w_g, w_down, conv_w, conv_b, *, prefix=None, s0=None, s1=None):
    m, d = x.shape
    d_ff = w_down.shape[0]
    sequential = prefix is not None
    tm = _pick(m, 512)
    tf = _pick(d_ff, 512)
    n_f = d_ff // tf
    n_t = m // tm
    g_off = 0 if sequential else n_f
    col_a = lambda r, w: pl.BlockSpec((r, w), lambda t, f: (0, f))
    col_g = lambda r, w: pl.BlockSpec((r, w), lambda t, f: (0, f + n_f))
    wd_spec = pl.BlockSpec((tf, d), lambda t, f: (f, 0))
    in_specs = [pl.BlockSpec((tm, d), lambda t, f: (t, 0)),
                pl.BlockSpec((1, d), lambda t, f: (0, 0)),
                pl.BlockSpec((1, d), lambda t, f: (0, 0)),
                col_a(d, tf), pl.BlockSpec((d, tf), lambda t, f: (0, f + g_off)), wd_spec,
                col_a(CONV_W, tf), col_g(CONV_W, tf), col_a(1, tf), col_g(1, tf)]
    args = [x, g_ffn, g_final, w_a, w_g, w_down, conv_w, conv_w, conv_b, conv_b]
    scratch = [pltpu.VMEM((tm, d), BF16), pltpu.VMEM((tm, d), F32)]
    out_specs = [pl.BlockSpec((tm, d), lambda t, f: (t, 0))]
    out_shape = [jax.ShapeDtypeStruct((m, d), F32)]
    if sequential:
        in_specs += [col_a(8, tf), col_g(8, tf)]
        args += [prefix, prefix]
        u_rows = 8 * n_t
        u_spec = pl.BlockSpec((8, tf), lambda t, f: (t, f))
        scratch += [pltpu.VMEM((2, n_f, 8, tf), F32), pltpu.VMEM((2, tm + 8, tf), F32)]
    else:
        assert n_t == 1, "the bf16 weight outputs are written once per weight block"
        rows_a = pl.BlockSpec((tm, tf), lambda t, f: (t, f))
        rows_g = pl.BlockSpec((tm, tf), lambda t, f: (t, f + n_f))
        in_specs += [rows_a, rows_a, rows_g, rows_g]
        args += [s0, s1, s0, s1]
        u_rows = m
        u_spec = rows_a
    out_specs += [u_spec, u_spec]
    out_shape += [jax.ShapeDtypeStruct((u_rows, d_ff), F32)] * 2
    if not sequential:
        out_specs += [col_a(d, tf), col_a(d, tf), wd_spec]
        out_shape += [jax.ShapeDtypeStruct((d, d_ff), BF16)] * 2 + [jax.ShapeDtypeStruct((d_ff, d), BF16)]
    y, ua, ug, *w_bf16 = pl.pallas_call(
        functools.partial(_ffn_kernel, sequential=sequential, tm=tm, tf=tf, n_f=n_f),
        grid=(n_t, n_f),
        in_specs=in_specs,
        out_specs=out_specs,
        out_shape=out_shape,
        scratch_shapes=scratch,
        compiler_params=_cparams(2),
        name="conv_ffn_seq" if sequential else "conv_ffn_rows",
    )(*args)
    return y, jnp.concatenate([ua, ug], axis=1), w_bf16 or None


def kernel(x_prompt, x_sample, cache_k, cache_v, page_table, state_C, state_n, state_m, state_conv,
           meta_tokens, g_mix, w_in, b_i, b_f, lam_q1, lam_k1, lam_q2, lam_k2, g_da, g_ml, w_out,
           g_ffn, w_up, conv_w, conv_b, w_down, g_final):
    depth = w_in.shape[0]
    assert depth == 1, "single-layer trunk"
    assert x_prompt.shape[0] == 1 and x_sample.shape[1] == 1
    seq, d = x_prompt.shape[1:]
    n_dec = x_sample.shape[0]
    assert N_META + n_dec <= TAIL_ROWS
    d_ff = w_down.shape[1]
    n_pool = cache_k.shape[1]
    gate_off = 7 * GROUP_W
    s_lo, s_hi = N_META, N_META + n_dec

    x_main = x_prompt[0]
    x_tail = jnp.concatenate([meta_tokens.astype(F32), x_sample[:, 0],
                              jnp.zeros((TAIL_ROWS - s_hi, d), F32)], axis=0)
    w_in0 = jnp.swapaxes(w_in[0], 0, 1)
    wg = jnp.zeros((2 * HEAD_W, d), F32)
    wg = wg.at[:N_HEADS].set(w_in0[gate_off:gate_off + N_HEADS])
    wg = wg.at[HEAD_W:HEAD_W + N_HEADS].set(w_in0[gate_off + N_HEADS:gate_off + 2 * N_HEADS])
    wg = wg.astype(BF16)
    gbias = jnp.zeros((1, 2 * HEAD_W), F32)
    gbias = gbias.at[0, :N_HEADS].set(b_i[0]).at[0, HEAD_W:HEAD_W + N_HEADS].set(b_f[0])
    lam4 = jnp.stack([lam_q1[0], lam_k1[0], lam_q2[0], lam_k2[0]]).astype(F32)
    g_da2 = g_da[0].reshape(1, HEAD_W)
    g_ml2 = g_ml[0].reshape(1, GROUP_W)
    g_mix2 = g_mix[0].reshape(1, d)
    g_ffn2 = g_ffn[0].reshape(1, d)
    g_fin2 = g_final.reshape(1, d)
    conv_w2 = conv_w[0]
    conv_b2 = conv_b[0].reshape(1, 2 * d_ff)
    k_scale = HEAD_W ** -0.5
    q_scale = DA_QK ** -0.5 * LOG2E

    other_groups = ((0, 3, 4, 5, 6), (q_scale, 1.0, k_scale, 1.0, 1.0))

    h_t, ig_t, lf_t = _pre(x_tail, g_mix2, wg, gbias)
    kv_f_t, kv_b_t = _proj(h_t, w_in0, (1, 2), (1.0, 1.0), True)
    oth_t = _proj(h_t, w_in0, *other_groups, False)
    pt = dict(kv_b=kv_b_t, oth=oth_t, ig=ig_t, lf=lf_t, k_f=kv_f_t[0], v_f=kv_f_t[1],
              dq=oth_t[0], mq=oth_t[1], mk=oth_t[2], mv=oth_t[3], mo=oth_t[4])

    h_m, ig_m, lf_m = _pre(x_main, g_mix2, wg, gbias)
    k_rows, k_b_m = _proj_rows(h_m, w_in0, 1, pt["k_f"][:N_META])
    v_rows, v_b_m = _proj_rows(h_m, w_in0, 2, pt["v_f"][:N_META])
    pm = dict(oth=_proj(h_m, w_in0, *other_groups, False), ig=ig_m, lf=lf_m)

    tk = _pick(seq, 512)
    tq = _pick(seq, 1024)
    pairs = [(i, j) for i in range(seq // tq) for j in range((i + 1) * tq // tk)]
    kv_main = ((k_b_m[None], 0), (v_b_m[None], 0))
    kv_meta = ((pt["kv_b"], 0), (pt["kv_b"], 1))
    da_main = _attn((pm["oth"], 0), *kv_main, *kv_meta, lam4, g_da2,
                    n_q_rows=seq, tq=tq, tk=tk, q_pos0=0, prefix_off=FAR, pairs=pairs)
    da_meta = _attn((pt["oth"], 0), *kv_main, *kv_meta, lam4, g_da2,
                    n_q_rows=N_META, tq=N_META, tk=tk, q_pos0=-FAR, prefix_off=0, pairs=[(0, 0)])

    c_zero = jnp.zeros((N_HEADS, HEAD_W, 2 * HEAD_W), F32)
    m_zero = jnp.zeros((1, HEAD_W), F32)
    ml_meta, c_meta, m_meta = _mlstm(pt["oth"], pt["ig"], pt["lf"], c_zero, m_zero, g_ml2,
                                     n_rows=N_META, chunk=N_META)
    ml_main, c_fin, m_fin = _mlstm(pm["oth"], pm["ig"], pm["lf"], c_meta, m_meta, g_ml2,
                                   n_rows=seq, chunk=_pick(seq, 256))

    f32_rows = lambda a: a[s_lo:s_hi].astype(F32)
    heads = lambda a: a.reshape(n_dec, N_HEADS, HEAD_W)
    twice = lambda a: jnp.concatenate([heads(a), heads(a)], axis=1)
    ck = cache_k[0].reshape(n_pool, PAGE_SIZE * N_HEADS, HEAD_W)
    cv = cache_v[0].reshape(n_pool, PAGE_SIZE * N_HEADS, HEAD_W)
    da_smp = _decode_attn(page_table, twice(f32_rows(pt["dq"])), twice(pt["k_f"][s_lo:s_hi]),
                          twice(pt["v_f"][s_lo:s_hi]), ck, cv, lam4, g_da2)
    da_smp = da_smp.reshape(n_dec, GROUP_W).astype(BF16)
    m_pad = jnp.zeros((n_dec, 1, HEAD_W), F32).at[:, 0, :N_HEADS].set(state_m[0].astype(F32))
    ml_smp, c_smp, n_smp, m_smp = _mlstm_step(
        heads(f32_rows(pt["mq"])), heads(f32_rows(pt["mk"])), heads(f32_rows(pt["mv"])),
        heads(pt["mo"][s_lo:s_hi]), pt["ig"][s_lo:s_hi, None, :], pt["lf"][s_lo:s_hi, None, :],
        state_C[0].astype(F32), state_n[0].astype(F32), m_pad, g_ml2.reshape(N_HEADS, HEAD_W))
    ml_smp = ml_smp.reshape(n_dec, GROUP_W).astype(BF16)

    pad_rows = jnp.zeros((TAIL_ROWS - s_hi, GROUP_W), BF16)
    da_tail = jnp.concatenate([da_meta, da_smp, pad_rows], axis=0)
    ml_tail = jnp.concatenate([ml_meta, ml_smp, pad_rows], axis=0)
    x1_main = _out_proj(da_main, ml_main, x_main, w_out[0])
    x1_tail = _out_proj(da_tail, ml_tail, x_tail, w_out[0])

    hist = jnp.zeros((2, TAIL_ROWS, 2 * d_ff), F32).at[:, s_lo:s_hi].set(
        jnp.swapaxes(state_conv[0].astype(F32), 0, 1))
    y_tail, u_tail, (wa_b, wg_b, wd_b) = _ffn(x1_tail, g_ffn2, g_fin2, w_up[0], w_up[0], w_down[0],
                                              conv_w2, conv_b2, s0=hist[0], s1=hist[1])
    y_main, u_last, _ = _ffn(x1_main, g_ffn2, g_fin2, wa_b, wg_b, wd_b, conv_w2, conv_b2,
                             prefix=u_tail[N_META - 8:N_META])

    y_prompt = y_main[None]
    y_sample = y_tail[s_lo:s_hi, None, :]
    k_rows_p = k_rows[None, None]
    v_rows_p = v_rows[None, None]
    k_rows_s = pt["k_f"][s_lo:s_hi].reshape(1, n_dec, 1, N_HEADS, HEAD_W)
    v_rows_s = pt["v_f"][s_lo:s_hi].reshape(1, n_dec, 1, N_HEADS, HEAD_W)
    c_p = c_fin[None, None, :, :, :HEAD_W]
    n_p = c_fin[None, None, :, :, HEAD_W]
    m_p = m_fin[None, :, :N_HEADS]
    c_s = c_smp[None]
    n_s = n_smp[None]
    m_s = m_smp[None, :, 0, :N_HEADS]
    conv_p = u_last[None, None, -2:, :]
    conv_s = jnp.stack([state_conv[0][:, 1].astype(F32), u_tail[s_lo:s_hi]], axis=1)[None]
    return (y_prompt, y_sample, k_rows_p, v_rows_p, k_rows_s, v_rows_s,
            c_p, n_p, m_p, c_s, n_s, m_s, conv_p, conv_s)
```

```python
import functools

import jax
import jax.numpy as jnp
from jax import lax
from jax.experimental import pallas as pl
from jax.experimental.pallas import tpu as pltpu

F32 = jnp.float32
BF16 = jnp.bfloat16

N_META = 16
N_HEADS = 8
HEAD_W = 128
DA_QK = 64
PAGE_SIZE = 128
CONV_W = 3
EPS = 1e-6
GROUP_W = N_HEADS * HEAD_W
LAM_INIT = 0.8 - 0.6 * 1.0
TAIL_ROWS = 128
NEG_INF = float("-inf")
LOG2E = 1.4426950408889634
FAR = 1 << 24

V7X_VMEM_BYTES = 64 * 1024 * 1024
VMEM_LIMIT = V7X_VMEM_BYTES * 13 // 16

_NT = (((1,), (1,)), ((), ()))


def _cparams(n_axes):
    return pltpu.CompilerParams(dimension_semantics=("arbitrary",) * n_axes,
                                vmem_limit_bytes=VMEM_LIMIT)


def _pick(total, pref):
    t = min(pref, total)
    while total % t:
        t //= 2
    return t


def _pre_kernel(x_ref, g_ref, wg_ref, bias_ref, h_ref, ig_ref, lf_ref):
    x = x_ref[...]
    y = x * lax.rsqrt(jnp.mean(x * x, axis=-1, keepdims=True) + EPS) * g_ref[...]
    hb = y.astype(BF16)
    h_ref[...] = hb
    z = lax.dot_general(hb, wg_ref[...], _NT, preferred_element_type=F32) + bias_ref[...]
    ig_ref[...] = z[:, :HEAD_W]
    zf = z[:, HEAD_W:]
    lf_ref[...] = jnp.minimum(zf, 0.0) - jnp.log1p(jnp.exp(-jnp.abs(zf)))


def _pre(x, g, wg, bias):
    m, d = x.shape
    tm = _pick(m, 512)
    return pl.pallas_call(
        _pre_kernel,
        grid=(m // tm,),
        in_specs=[pl.BlockSpec((tm, d), lambda i: (i, 0)),
                  pl.BlockSpec((1, d), lambda i: (0, 0)),
                  pl.BlockSpec((2 * HEAD_W, d), lambda i: (0, 0)),
                  pl.BlockSpec((1, 2 * HEAD_W), lambda i: (0, 0))],
        out_specs=[pl.BlockSpec((tm, d), lambda i: (i, 0)),
                   pl.BlockSpec((tm, HEAD_W), lambda i: (i, 0)),
                   pl.BlockSpec((tm, HEAD_W), lambda i: (i, 0))],
        out_shape=[jax.ShapeDtypeStruct((m, d), BF16),
                   jax.ShapeDtypeStruct((m, HEAD_W), F32),
                   jax.ShapeDtypeStruct((m, HEAD_W), F32)],
        compiler_params=_cparams(1),
        name="pre_norm_gates",
    )(x, g, wg, bias)


def _proj_kernel(h_ref, w_ref, s_ref, *rest, with_f32):
    if with_f32:
        of_ref, ob_ref, wb_ref = rest
    else:
        ob_ref, wb_ref = rest

    @pl.when(pl.program_id(1) == 0)
    def _():
        wb_ref[...] = w_ref[...].astype(BF16)

    z = lax.dot_general(h_ref[...], wb_ref[...], _NT, preferred_element_type=F32)
    if with_f32:
        for hd in range(N_HEADS):
            of_ref[0, :, hd, :] = z[:, hd * HEAD_W:(hd + 1) * HEAD_W]
    ob_ref[0] = (z * s_ref[0, :, 0:1]).astype(BF16)


def _proj(h, w_t, groups, scales, with_f32):
    m, d = h.shape
    tm = _pick(m, 1024)
    ng = len(groups)
    first, rest_off = groups[0], groups[1] - 1 if ng > 1 else 0
    assert list(groups) == [first] + [rest_off + k for k in range(1, ng)]
    s_arr = jnp.broadcast_to(jnp.asarray(scales, F32)[:, None, None], (ng, 1, HEAD_W))

    def w_map(j, i):
        return (jnp.where(j == 0, first, j + rest_off), 0)

    out_specs = [pl.BlockSpec((1, tm, GROUP_W), lambda j, i: (j, i, 0))]
    out_shape = [jax.ShapeDtypeStruct((ng, m, GROUP_W), BF16)]
    if with_f32:
        out_specs = [pl.BlockSpec((1, tm, N_HEADS, HEAD_W), lambda j, i: (j, i, 0, 0))] + out_specs
        out_shape = [jax.ShapeDtypeStruct((ng, m, N_HEADS, HEAD_W), F32)] + out_shape
    outs = pl.pallas_call(
        functools.partial(_proj_kernel, with_f32=with_f32),
        grid=(ng, m // tm),
        in_specs=[pl.BlockSpec((tm, d), lambda j, i: (i, 0)),
                  pl.BlockSpec((GROUP_W, d), w_map),
                  pl.BlockSpec((1, 1, HEAD_W), lambda j, i: (j, 0, 0))],
        out_specs=out_specs,
        out_shape=out_shape,
        scratch_shapes=[pltpu.VMEM((GROUP_W, d), BF16)],
        compiler_params=_cparams(2),
        name="in_proj_f32" if with_f32 else "in_proj",
    )(h, w_t, s_arr)
    return outs if with_f32 else outs[0]


def _proj_rows_kernel(h_ref, w_ref, meta_ref, rows_out, ob_ref, wb_ref, stage, sems, meta_sem,
                      *, tm, n_i, row0):
    i = pl.program_id(0)
    slot = i % 2

    def row_copy(sl, blk):
        return pltpu.make_async_copy(stage.at[sl], rows_out.at[pl.ds(row0 + blk * tm, tm)], sems.at[sl])

    @pl.when(i == 0)
    def _():
        wb_ref[...] = w_ref[...].astype(BF16)
        meta_copy = pltpu.make_async_copy(meta_ref, rows_out.at[pl.ds(0, row0)], meta_sem.at[0])
        meta_copy.start()
        meta_copy.wait()

    @pl.when(i >= 2)
    def _():
        row_copy(slot, i - 2).wait()

    z = lax.dot_general(h_ref[...], wb_ref[...], _NT, preferred_element_type=F32)
    for hd in range(N_HEADS):
        stage[slot, :, hd, :] = z[:, hd * HEAD_W:(hd + 1) * HEAD_W]
    ob_ref[...] = z.astype(BF16)
    row_copy(slot, i).start()

    @pl.when(i == n_i - 1)
    def _():
        row_copy(slot, i).wait()
        if n_i >= 2:
            row_copy(1 - slot, i - 1).wait()


def _proj_rows(h, w_t, group, meta_rows):
    m, d = h.shape
    row0 = meta_rows.shape[0]
    tm = _pick(m, 1024)
    n_i = m // tm
    return pl.pallas_call(
        functools.partial(_proj_rows_kernel, tm=tm, n_i=n_i, row0=row0),
        grid=(n_i,),
        in_specs=[pl.BlockSpec((tm, d), lambda i: (i, 0)),
                  pl.BlockSpec((GROUP_W, d), lambda i: (group, 0)),
                  pl.BlockSpec((row0, N_HEADS, HEAD_W), lambda i: (0, 0, 0))],
        out_specs=[pl.BlockSpec(memory_space=pl.ANY),
                   pl.BlockSpec((tm, GROUP_W), lambda i: (i, 0))],
        out_shape=[jax.ShapeDtypeStruct((row0 + m, N_HEADS, HEAD_W), F32),
                   jax.ShapeDtypeStruct((m, GROUP_W), BF16)],
        scratch_shapes=[pltpu.VMEM((GROUP_W, d), BF16),
                        pltpu.VMEM((2, tm, N_HEADS, HEAD_W), F32),
                        pltpu.SemaphoreType.DMA((2,)),
                        pltpu.SemaphoreType.DMA((1,))],
        compiler_params=_cparams(1),
        name="in_proj_rows",
    )(h, w_t, meta_rows)


def _attn_kernel(qi_ref, kj_ref, q_ref, k_ref, v_ref, kp_ref, vp_ref, lam_ref, g_ref,
                 o_ref, m_sc, l_sc, acc_sc, *, tq, tk, sub, ahead, q_pos0, prefix_off, n_steps,
                 mask_deltas, hps):
    step = pl.program_id(1)
    i = qi_ref[step]
    j = kj_ref[step]
    lane = lax.broadcasted_iota(jnp.int32, (tq, HEAD_W), 1)
    hsl = [slice(hd * HEAD_W, (hd + 1) * HEAD_W) for hd in range(hps)]
    q2 = []
    for hs in hsl:
        q = q_ref[:, hs]
        zero = jnp.zeros_like(q)
        q2.append(jnp.concatenate([jnp.where(lane < DA_QK, q, zero), jnp.where(lane >= DA_QK, q, zero)],
                                  axis=0))
    n_sub = 2 * tq // sub

    @pl.when(j == 0)
    def _():
        for hd, hs in enumerate(hsl):
            own = slice(hd * 2 * tq, (hd + 1) * 2 * tq)
            sp = lax.dot_general(q2[hd], kp_ref[:, hs], _NT, preferred_element_type=F32)
            col = lax.broadcasted_iota(jnp.int32, sp.shape, 1)
            r = lax.broadcasted_iota(jnp.int32, sp.shape, 0)
            ok = (col < N_META) & (col <= jnp.where(r >= tq, r - tq, r) + prefix_off)
            sp = jnp.where(ok, sp, NEG_INF)
            m0 = jnp.max(sp, axis=1, keepdims=True)
            p = jnp.exp2(sp - m0)
            m_sc[own] = jnp.broadcast_to(m0, (2 * tq, HEAD_W))
            l_sc[own] = jnp.broadcast_to(jnp.sum(p, axis=1, keepdims=True), (2 * tq, HEAD_W))
            acc_sc[own] = jnp.dot(p.astype(BF16), vp_ref[:, hs], preferred_element_type=F32)

    def block(delta):
        def first_visible(r):
            return FAR if delta is None else delta + (r * sub) % tq
        live = [(hd, r) for hd in range(hps) for r in range(n_sub) if first_visible(r) + sub - 1 >= 0]
        if not live:
            return
        k = [k_ref[:, hs] for hs in hsl]
        ones = jnp.ones((tk, HEAD_W), BF16)
        v_aug = [jnp.concatenate([v_ref[:, hs], ones], axis=1) for hs in hsl]
        rel = (lax.broadcasted_iota(jnp.int32, (sub, tk), 1)
               - lax.broadcasted_iota(jnp.int32, (sub, tk), 0))
        m_all, l_all, acc_all = m_sc[...], l_sc[...], acc_sc[...]
        new = []

        def scores(unit):
            hd, r = unit
            return lax.dot_general(q2[hd][r * sub:(r + 1) * sub], k[hd], _NT, preferred_element_type=F32)

        def state_rows(unit):
            hd, r = unit
            return slice(hd * 2 * tq + r * sub, hd * 2 * tq + (r + 1) * sub)

        pending = [scores(u) for u in live[:ahead]]
        for n, (hd, r) in enumerate(live):
            rows = state_rows((hd, r))
            s = pending.pop(0)
            if n + ahead < len(live):
                pending.append(scores(live[n + ahead]))
            if first_visible(r) < tk - 1:
                s = jnp.where(rel <= first_visible(r), s, NEG_INF)
            m_prev = m_all[rows]
            m_new = jnp.maximum(m_prev, jnp.max(s, axis=1, keepdims=True))
            alpha = jnp.exp2(m_prev - m_new)
            p = jnp.exp2(s - jnp.concatenate([m_new] * (tk // HEAD_W), axis=1))
            pv = jnp.dot(p.astype(BF16), v_aug[hd], preferred_element_type=F32)
            new.append((m_new, alpha * l_all[rows] + pv[:, HEAD_W:], alpha * acc_all[rows] + pv[:, :HEAD_W]))
        for unit, (m_new, l_new, acc_new) in zip(live, new):
            rows = state_rows(unit)
            m_sc[rows] = m_new
            l_sc[rows] = l_new
            acc_sc[rows] = acc_new

    delta_now = q_pos0 + i * tq - j * tk
    pl.when(delta_now >= tk - 1)(lambda: block(None))
    for delta in mask_deltas:
        pl.when(delta_now == delta)(functools.partial(block, delta))

    is_last = jnp.logical_or(step == n_steps - 1, qi_ref[step + 1] != i)

    @pl.when(is_last)
    def _():
        o = acc_sc[...] / l_sc[...]
        lam = (jnp.exp(jnp.sum(lam_ref[0:1, :] * lam_ref[1:2, :], axis=1, keepdims=True))
               - jnp.exp(jnp.sum(lam_ref[2:3, :] * lam_ref[3:4, :], axis=1, keepdims=True)) + LAM_INIT)
        for hd, hs in enumerate(hsl):
            base = hd * 2 * tq
            d = o[base:base + tq] - lam * o[base + tq:base + 2 * tq]
            y = d * lax.rsqrt(jnp.mean(d * d, axis=-1, keepdims=True) + EPS) * g_ref[...]
            o_ref[:, hs] = (y * (1.0 - LAM_INIT)).astype(o_ref.dtype)


def _attn(q, k, v, kp, vp, lam4, g_da, *, n_q_rows, tq, tk, q_pos0, prefix_off, pairs, hps=4):
    qi = jnp.asarray([p[0] for p in pairs] + [-1], jnp.int32)
    kj = jnp.asarray([p[1] for p in pairs] + [0], jnp.int32)
    n_steps = len(pairs)
    deltas = {q_pos0 + pi * tq - pj * tk for pi, pj in pairs}
    kern = functools.partial(_attn_kernel, tq=tq, tk=tk, sub=min(256, 2 * tq), ahead=3, q_pos0=q_pos0,
                             prefix_off=prefix_off, n_steps=n_steps,
                             mask_deltas=tuple(sorted(dl for dl in deltas if dl < tk - 1)), hps=hps)
    hw = hps * HEAD_W
    grid_spec = pltpu.PrefetchScalarGridSpec(
        num_scalar_prefetch=2,
        grid=(N_HEADS // hps, n_steps),
        in_specs=[pl.BlockSpec((None, tq, hw), lambda h, s, qi, kj: (q[1], qi[s], h)),
                  pl.BlockSpec((None, tk, hw), lambda h, s, qi, kj: (k[1], kj[s], h)),
                  pl.BlockSpec((None, tk, hw), lambda h, s, qi, kj: (v[1], kj[s], h)),
                  pl.BlockSpec((None, TAIL_ROWS, hw), lambda h, s, qi, kj: (kp[1], 0, h)),
                  pl.BlockSpec((None, TAIL_ROWS, hw), lambda h, s, qi, kj: (vp[1], 0, h)),
                  pl.BlockSpec((4, DA_QK), lambda h, s, qi, kj: (0, 0)),
                  pl.BlockSpec((1, HEAD_W), lambda h, s, qi, kj: (0, 0))],
        out_specs=pl.BlockSpec((tq, hw), lambda h, s, qi, kj: (qi[s], h)),
        scratch_shapes=[pltpu.VMEM((hps * 2 * tq, HEAD_W), F32)] * 3)
    return pl.pallas_call(
        kern, grid_spec=grid_spec,
        out_shape=jax.ShapeDtypeStruct((n_q_rows, GROUP_W), BF16),
        compiler_params=_cparams(2),
        name="diff_attn_prompt",
    )(qi, kj, q[0], k[0], v[0], kp[0], vp[0], lam4, g_da)


def _mlstm_kernel(q_ref, k_ref, v_ref, o_ref, ig_ref, lf_ref, c0_ref, m0_ref, g_ref,
                  y_ref, c_out_ref, m_out_ref, c_sc, m_sc, *, chunk, n_chunks):
    c = pl.program_id(0)

    @pl.when(c == 0)
    def _():
        c_sc[...] = c0_ref[...]
        m_sc[...] = m0_ref[...]

    L = chunk
    ig = ig_ref[...]
    lf = lf_ref[...]
    row = lax.broadcasted_iota(jnp.int32, (L, L), 0)
    col = lax.broadcasted_iota(jnp.int32, (L, L), 1)
    causal = col <= row
    b_col = jnp.dot(causal.astype(F32), lf, preferred_element_type=F32,
                    precision=lax.Precision.HIGHEST)
    b_row = b_col.T
    ig_row = ig.T
    m_prev_all = m_sc[...]
    lane1 = lax.broadcasted_iota(jnp.int32, (1, HEAD_W), 1)
    m_next_all = m_prev_all
    ones = jnp.ones((L, HEAD_W), BF16)

    heads = range(N_HEADS)
    hsl = [slice(h * HEAD_W, (h + 1) * HEAD_W) for h in heads]
    c_old = [c_sc[h] for h in heads]
    q_l = [q_ref[:, hs] for hs in hsl]
    k_l = [k_ref[:, hs] for hs in hsl]
    v_aug = [jnp.concatenate([v_ref[:, hs], ones], axis=1) for hs in hsl]
    s_l = [lax.dot_general(q_l[h], k_l[h], _NT, preferred_element_type=F32) for h in heads]
    qc_l = [jnp.dot(q_l[h], c_old[h].astype(BF16), preferred_element_type=F32) for h in heads]

    gate = []
    for h in heads:
        bc = b_col[:, h:h + 1]
        m_prev = m_prev_all[:, h:h + 1]
        dm = jnp.where(causal, ig_row[h:h + 1, :] - b_row[h:h + 1, :], NEG_INF)
        rel_m = jnp.maximum(m_prev, jnp.max(dm, axis=1, keepdims=True))
        m_t = bc + rel_m
        a = (s_l[h] * jnp.exp(dm - rel_m)).astype(BF16)
        gate.append((bc, m_prev, m_t, jnp.exp(m_prev - rel_m), a))

    num_l = [jnp.dot(gate[h][4], v_aug[h], preferred_element_type=F32) + gate[h][3] * qc_l[h]
             for h in heads]

    kw_l = []
    for h in heads:
        bc, m_prev, m_t, _, _ = gate[h]
        b_last = bc[L - 1:L]
        m_new = m_t[L - 1:L]
        ws = jnp.exp(b_last - bc + ig[:, h:h + 1] - m_new)
        kw_l.append(((k_l[h].astype(F32) * ws).T.astype(BF16),
                     jnp.exp(b_last + m_prev - m_new)))
        m_next_all = jnp.where(lane1 == h, m_new, m_next_all)

    for h in heads:
        kw_t, decay = kw_l[h]
        c_sc[h] = decay * c_old[h] + jnp.dot(kw_t, v_aug[h], preferred_element_type=F32)

    for h in heads:
        hs = hsl[h]
        num = num_l[h][:, :HEAD_W]
        den = num_l[h][:, HEAD_W:]
        hh = num / jnp.maximum(jnp.abs(den), jnp.exp(-gate[h][2]))
        y = hh * lax.rsqrt(jnp.mean(hh * hh, axis=-1, keepdims=True) + EPS) * g_ref[:, hs]
        y_ref[:, hs] = (y * jax.nn.sigmoid(o_ref[:, hs].astype(F32))).astype(y_ref.dtype)

    m_sc[...] = m_next_all

    @pl.when(c == n_chunks - 1)
    def _():
        c_out_ref[...] = c_sc[...]
        m_out_ref[...] = m_sc[...]


def _mlstm(proj, ig, lf, c0, m0, g_ml, *, n_rows, chunk):
    n_chunks = n_rows // chunk
    row_blk = lambda w: pl.BlockSpec((chunk, w), lambda c: (c, 0))
    grp_blk = lambda g: pl.BlockSpec((None, chunk, GROUP_W), lambda c: (g, c, 0))
    whole = lambda shape: pl.BlockSpec(shape, lambda c: (0,) * len(shape))
    return pl.pallas_call(
        functools.partial(_mlstm_kernel, chunk=chunk, n_chunks=n_chunks),
        grid=(n_chunks,),
        in_specs=[grp_blk(1), grp_blk(2), grp_blk(3), grp_blk(4),
                  row_blk(HEAD_W), row_blk(HEAD_W),
                  whole((N_HEADS, HEAD_W, 2 * HEAD_W)), whole((1, HEAD_W)), whole((1, GROUP_W))],
        out_specs=[row_blk(GROUP_W), whole((N_HEADS, HEAD_W, 2 * HEAD_W)), whole((1, HEAD_W))],
        out_shape=[jax.ShapeDtypeStruct((n_rows, GROUP_W), BF16),
                   jax.ShapeDtypeStruct((N_HEADS, HEAD_W, 2 * HEAD_W), F32),
                   jax.ShapeDtypeStruct((1, HEAD_W), F32)],
        scratch_shapes=[pltpu.VMEM((N_HEADS, HEAD_W, 2 * HEAD_W), F32), pltpu.VMEM((1, HEAD_W), F32)],
        compiler_params=_cparams(1),
        name="mlstm_prompt",
    )(proj, proj, proj, proj, ig, lf, c0, m0, g_ml)


def _decode_kernel(pt_ref, q_ref, kn_ref, vn_ref, lam_ref, g_ref, *rest, pages_per_step, n_steps):
    k_refs = rest[:pages_per_step]
    v_refs = rest[pages_per_step:2 * pages_per_step]
    o_ref, m_sc, l_sc, acc_sc = rest[2 * pages_per_step:]
    j = pl.program_id(1)
    n_rows = 2 * N_HEADS
    page_rows = PAGE_SIZE * N_HEADS
    q16 = q_ref[0]
    rr = lax.broadcasted_iota(jnp.int32, (n_rows, HEAD_W), 0)
    cc = lax.broadcasted_iota(jnp.int32, (n_rows, HEAD_W), 1)
    qm = jnp.where(cc // DA_QK == rr // N_HEADS, q16, 0.0)

    @pl.when(j == 0)
    def _():
        m_sc[...] = jnp.broadcast_to(jnp.sum(qm * kn_ref[0], axis=1, keepdims=True), m_sc.shape)
        l_sc[...] = jnp.ones_like(l_sc)
        acc_sc[...] = vn_ref[0]

    qb = qm.astype(BF16)
    key_head = lax.broadcasted_iota(jnp.int32, (n_rows, page_rows), 1) % N_HEADS
    row_head = lax.broadcasted_iota(jnp.int32, (n_rows, page_rows), 0) % N_HEADS
    own = key_head == row_head
    s_pages = [jnp.where(own, lax.dot_general(qb, kr[0].astype(BF16), _NT, preferred_element_type=F32),
                         NEG_INF) for kr in k_refs]
    s_max = s_pages[0]
    for s in s_pages[1:]:
        s_max = jnp.maximum(s_max, s)
    m_prev = m_sc[...]
    m_new = jnp.maximum(m_prev, jnp.max(s_max, axis=1, keepdims=True))
    alpha = jnp.exp2(m_prev - m_new)
    m_rep = jnp.concatenate([m_new] * (page_rows // HEAD_W), axis=1)
    l_add = jnp.zeros((n_rows, 1), F32)
    pv = jnp.zeros((n_rows, HEAD_W), F32)
    for s, vr in zip(s_pages, v_refs):
        p = jnp.exp2(s - m_rep)
        l_add += jnp.sum(p, axis=1, keepdims=True)
        pv += jnp.dot(p.astype(BF16), vr[0].astype(BF16), preferred_element_type=F32)
    l_sc[...] = alpha * l_sc[...] + l_add
    acc_sc[...] = alpha * acc_sc[...] + pv
    m_sc[...] = m_new

    @pl.when(j == n_steps - 1)
    def _():
        o = acc_sc[...] / l_sc[...]
        lam = (jnp.exp(jnp.sum(lam_ref[0:1, :] * lam_ref[1:2, :], axis=1, keepdims=True))
               - jnp.exp(jnp.sum(lam_ref[2:3, :] * lam_ref[3:4, :], axis=1, keepdims=True)) + LAM_INIT)
        d = o[:N_HEADS] - lam * o[N_HEADS:]
        y = d * lax.rsqrt(jnp.mean(d * d, axis=-1, keepdims=True) + EPS) * g_ref[...]
        o_ref[0] = y * (1.0 - LAM_INIT)


def _decode_attn(page_table, q, k_new, v_new, cache_k, cache_v, lam4, g_da):
    b, n_pages = page_table.shape
    pps = _pick(n_pages, 16)
    n_steps = n_pages // pps
    pt_flat = page_table.reshape(-1)
    n_rows = 2 * N_HEADS

    def page_spec(g):
        return pl.BlockSpec((1, PAGE_SIZE * N_HEADS, HEAD_W),
                            lambda bi, j, pt: (pt[bi * n_pages + j * pps + g], 0, 0))

    seq_spec = pl.BlockSpec((1, n_rows, HEAD_W), lambda bi, j, pt: (bi, 0, 0))
    grid_spec = pltpu.PrefetchScalarGridSpec(
        num_scalar_prefetch=1,
        grid=(b, n_steps),
        in_specs=[seq_spec, seq_spec, seq_spec,
                  pl.BlockSpec((4, DA_QK), lambda bi, j, pt: (0, 0)),
                  pl.BlockSpec((1, HEAD_W), lambda bi, j, pt: (0, 0))]
                 + [page_spec(g) for g in range(pps)] + [page_spec(g) for g in range(pps)],
        out_specs=pl.BlockSpec((1, N_HEADS, HEAD_W), lambda bi, j, pt: (bi, 0, 0)),
        scratch_shapes=[pltpu.VMEM((n_rows, HEAD_W), F32), pltpu.VMEM((n_rows, HEAD_W), F32),
                        pltpu.VMEM((n_rows, HEAD_W), F32)])
    return pl.pallas_call(
        functools.partial(_decode_kernel, pages_per_step=pps, n_steps=n_steps),
        grid_spec=grid_spec,
        out_shape=jax.ShapeDtypeStruct((b, N_HEADS, HEAD_W), F32),
        compiler_params=_cparams(2),
        name="diff_attn_decode",
    )(pt_flat, q, k_new, v_new, lam4, g_da, *([cache_k] * pps), *([cache_v] * pps))


def _mlstm_step_kernel(q_ref, k_ref, v_ref, o_ref, ig_ref, lf_ref, c_ref, n_ref, m_ref, g_ref,
                       y_ref, c_out_ref, n_out_ref, m_out_ref):
    q8 = q_ref[0]
    k8 = k_ref[0]
    v8 = v_ref[0]
    n8 = n_ref[0]
    ig = ig_ref[0]
    lf = lf_ref[0]
    m = m_ref[0]
    inter = lf + m
    m_t = jnp.maximum(inter, ig)
    w_all = jnp.exp(ig - m_t)
    g_all = jnp.exp(inter - m_t)
    floor_all = jnp.exp(-m_t)
    m_out_ref[0] = m_t
    q_t = q8.T
    k_t = k8.T
    qk = jnp.sum(q8 * k8, axis=1, keepdims=True)
    qn = jnp.sum(q8 * n8, axis=1, keepdims=True)
    for h in range(N_HEADS):
        w = w_all[:, h:h + 1]
        g = g_all[:, h:h + 1]
        c_h = c_ref[0, h]
        vh = v8[h:h + 1]
        a = qk[h:h + 1] * w
        q_c = jnp.sum(q_t[:, h:h + 1] * c_h, axis=0, keepdims=True)
        num = a * vh + g * q_c
        den = a + g * qn[h:h + 1]
        hh = num / jnp.maximum(jnp.abs(den), floor_all[:, h:h + 1])
        y = hh * lax.rsqrt(jnp.mean(hh * hh, axis=-1, keepdims=True) + EPS) * g_ref[h:h + 1]
        y_ref[0, h:h + 1] = (y * jax.nn.sigmoid(o_ref[0, h:h + 1].astype(F32))).astype(y_ref.dtype)
        c_out_ref[0, h] = g * c_h + w * (k_t[:, h:h + 1] * vh)
        n_out_ref[0, h:h + 1] = g * n8[h:h + 1] + w * k8[h:h + 1]


def _mlstm_step(q, k, v, o, ig, lf, c, n, m, g_ml):
    b = q.shape[0]
    hv = pl.BlockSpec((1, N_HEADS, HEAD_W), lambda i: (i, 0, 0))
    gate = pl.BlockSpec((1, 1, HEAD_W), lambda i: (i, 0, 0))
    cs = pl.BlockSpec((1, N_HEADS, HEAD_W, HEAD_W), lambda i: (i, 0, 0, 0))
    return pl.pallas_call(
        _mlstm_step_kernel,
        grid=(b,),
        in_specs=[hv, hv, hv, hv, gate, gate, cs, hv, gate,
                  pl.BlockSpec((N_HEADS, HEAD_W), lambda i: (0, 0))],
        out_specs=[hv, cs, hv, gate],
        out_shape=[jax.ShapeDtypeStruct((b, N_HEADS, HEAD_W), F32),
                   jax.ShapeDtypeStruct(c.shape, F32),
                   jax.ShapeDtypeStruct((b, N_HEADS, HEAD_W), F32),
                   jax.ShapeDtypeStruct((b, 1, HEAD_W), F32)],
        compiler_params=_cparams(1),
        name="mlstm_sample",
    )(q, k, v, o, ig, lf, c, n, m, g_ml)


def _out_proj_kernel(da_ref, ml_ref, x_ref, w_ref, o_ref, wb_ref):
    @pl.when(pl.program_id(1) == 0)
    def _():
        wb_ref[...] = w_ref[...].astype(BF16)

    y = jnp.concatenate([da_ref[...], ml_ref[...]], axis=1)
    o_ref[...] = x_ref[...] + jnp.dot(y, wb_ref[...], preferred_element_type=F32)


def _out_proj(da, ml, x, w_out):
    m, d = x.shape
    tm = _pick(m, 1024)
    tn = _pick(d, 1024)
    return pl.pallas_call(
        _out_proj_kernel,
        grid=(d // tn, m // tm),
        in_specs=[pl.BlockSpec((tm, GROUP_W), lambda j, i: (i, 0)),
                  pl.BlockSpec((tm, GROUP_W), lambda j, i: (i, 0)),
                  pl.BlockSpec((tm, tn), lambda j, i: (i, j)),
                  pl.BlockSpec((2 * GROUP_W, tn), lambda j, i: (0, j))],
        out_specs=pl.BlockSpec((tm, tn), lambda j, i: (i, j)),
        out_shape=jax.ShapeDtypeStruct((m, d), F32),
        scratch_shapes=[pltpu.VMEM((2 * GROUP_W, tn), BF16)],
        compiler_params=_cparams(2),
        name="out_proj",
    )(da, ml, x, w_out)


def _ffn_kernel(*refs, sequential, tm, tf, n_f):
    if sequential:
        (x_ref, gf_ref, gl_ref, wa_ref, wg_ref, wd_ref, cwa_ref, cwg_ref, cba_ref, cbg_ref,
         pa_ref, pg_ref, y_ref, ua_ref, ug_ref, h_sc, acc_sc, carry_sc, u_sc) = refs
    else:
        (x_ref, gf_ref, gl_ref, wa_ref, wg_ref, wd_ref, cwa_ref, cwg_ref, cba_ref, cbg_ref,
         s0a_ref, s1a_ref, s0g_ref, s1g_ref, y_ref, ua_ref, ug_ref, wab_ref, wgb_ref, wdb_ref,
         h_sc, acc_sc) = refs
        wab_ref[...] = wa_ref[...].astype(BF16)
        wgb_ref[...] = wg_ref[...].astype(BF16)
        wdb_ref[...] = wd_ref[...].astype(BF16)
        wa_ref, wg_ref, wd_ref = wab_ref, wgb_ref, wdb_ref
    t = pl.program_id(0)
    f = pl.program_id(1)
    w_sub = min(tf, 256)
    n_c = tf // w_sub

    @pl.when(f == 0)
    def _():
        x = x_ref[...]
        h_sc[...] = (x * lax.rsqrt(jnp.mean(x * x, axis=-1, keepdims=True) + EPS)
                     * gf_ref[...]).astype(BF16)
        acc_sc[...] = jnp.zeros_like(acc_sc)

    h = h_sc[...]

    if sequential:
        for which, p_ref in enumerate((pa_ref, pg_ref)):
            @pl.when(t == 0)
            def _():
                carry_sc[which, f] = p_ref[...]

            u_sc[which, 0:8, :] = carry_sc[which, f]

    def up(c):
        cs = slice(c * w_sub, (c + 1) * w_sub)
        return (jnp.dot(h, wa_ref[:, cs], preferred_element_type=F32),
                jnp.dot(h, wg_ref[:, cs], preferred_element_type=F32))

    def conv(u, which, c, cw_ref, cb_ref, u_out_ref):
        cs = slice(c * w_sub, (c + 1) * w_sub)
        if sequential:
            u_sc[which, 8:8 + tm, cs] = u
            s1 = u_sc[which, 7:7 + tm, cs]
            s0 = u_sc[which, 6:6 + tm, cs]
        else:
            s0 = (s0a_ref, s0g_ref)[which][:, cs]
            s1 = (s1a_ref, s1g_ref)[which][:, cs]
            u_out_ref[:, cs] = u
        return cb_ref[:, cs] + cw_ref[0:1, cs] * s0 + cw_ref[1:2, cs] * s1 + cw_ref[2:3, cs] * u

    pending = [up(0)]
    down = None
    for c in range(n_c):
        u_a, u_g = pending.pop(0)
        if c + 1 < n_c:
            pending.append(up(c + 1))
        ya = conv(u_a, 0, c, cwa_ref, cba_ref, ua_ref)
        yg = conv(u_g, 1, c, cwg_ref, cbg_ref, ug_ref)
        act = (ya * jax.nn.sigmoid(ya) * yg).astype(BF16)
        part = jnp.dot(act, wd_ref[c * w_sub:(c + 1) * w_sub, :], preferred_element_type=F32)
        down = part if down is None else down + part
    acc_sc[...] += down

    if sequential:
        for which, u_out_ref in enumerate((ua_ref, ug_ref)):
            tail8 = u_sc[which, tm:tm + 8, :]
            carry_sc[which, f] = tail8
            u_out_ref[...] = tail8

    @pl.when(f == n_f - 1)
    def _():
        x2 = x_ref[...] + acc_sc[...]
        y_ref[...] = x2 * lax.rsqrt(jnp.mean(x2 * x2, axis=-1, keepdims=True) + EPS) * gl_ref[...]


def _ffn(x, g_ffn, g_final, w_a, w_g, w_down, conv_w, conv_b, *, prefix=None, s0=None, s1=None):
    m, d = x.shape
    d_ff = w_down.shape[0]
    sequential = prefix is not None
    tm = _pick(m, 512)
    tf = _pick(d_ff, 512)
    n_f = d_ff // tf
    n_t = m // tm
    g_off = 0 if sequential else n_f
    col_a = lambda r, w: pl.BlockSpec((r, w), lambda t, f: (0, f))
    col_g = lambda r, w: pl.BlockSpec((r, w), lambda t, f: (0, f + n_f))
    wd_spec = pl.BlockSpec((tf, d), lambda t, f: (f, 0))
    in_specs = [pl.BlockSpec((tm, d), lambda t, f: (t, 0)),
                pl.BlockSpec((1, d), lambda t, f: (0, 0)),
                pl.BlockSpec((1, d), lambda t, f: (0, 0)),
                col_a(d, tf), pl.BlockSpec((d, tf), lambda t, f: (0, f + g_off)), wd_spec,
                col_a(CONV_W, tf), col_g(CONV_W, tf), col_a(1, tf), col_g(1, tf)]
    args = [x, g_ffn, g_final, w_a, w_g, w_down, conv_w, conv_w, conv_b, conv_b]
    scratch = [pltpu.VMEM((tm, d), BF16), pltpu.VMEM((tm, d), F32)]
    out_specs = [pl.BlockSpec((tm, d), lambda t, f: (t, 0))]
    out_shape = [jax.ShapeDtypeStruct((m, d), F32)]
    if sequential:
        in_specs += [col_a(8, tf), col_g(8, tf)]
        args += [prefix, prefix]
        u_rows = 8 * n_t
        u_spec = pl.BlockSpec((8, tf), lambda t, f: (t, f))
        scratch += [pltpu.VMEM((2, n_f, 8, tf), F32), pltpu.VMEM((2, tm + 8, tf), F32)]
    else:
        assert n_t == 1, "the bf16 weight outputs are written once per weight block"
        rows_a = pl.BlockSpec((tm, tf), lambda t, f: (t, f))
        rows_g = pl.BlockSpec((tm, tf), lambda t, f: (t, f + n_f))
        in_specs += [rows_a, rows_a, rows_g, rows_g]
        args += [s0, s1, s0, s1]
        u_rows = m
        u_spec = rows_a
    out_specs += [u_spec, u_spec]
    out_shape += [jax.ShapeDtypeStruct((u_rows, d_ff), F32)] * 2
    if not sequential:
        out_specs += [col_a(d, tf), col_a(d, tf), wd_spec]
        out_shape += [jax.ShapeDtypeStruct((d, d_ff), BF16)] * 2 + [jax.ShapeDtypeStruct((d_ff, d), BF16)]
    y, ua, ug, *w_bf16 = pl.pallas_call(
        functools.partial(_ffn_kernel, sequential=sequential, tm=tm, tf=tf, n_f=n_f),
        grid=(n_t, n_f),
        in_specs=in_specs,
        out_specs=out_specs,
        out_shape=out_shape,
        scratch_shapes=scratch,
        compiler_params=_cparams(2),
        name="conv_ffn_seq" if sequential else "conv_ffn_rows",
    )(*args)
    return y, jnp.concatenate([ua, ug], axis=1), w_bf16 or None


def kernel(x_prompt, x_sample, cache_k, cache_v, page_table, state_C, state_n, state_m, state_conv,
           meta_tokens, g_mix, w_in, b_i, b_f, lam_q1, lam_k1, lam_q2, lam_k2, g_da, g_ml, w_out,
           g_ffn, w_up, conv_w, conv_b, w_down, g_final):
    depth = w_in.shape[0]
    assert depth == 1, "single-layer trunk"
    assert x_prompt.shape[0] == 1 and x_sample.shape[1] == 1
    seq, d = x_prompt.shape[1:]
    n_dec = x_sample.shape[0]
    assert N_META + n_dec <= TAIL_ROWS
    d_ff = w_down.shape[1]
    n_pool = cache_k.shape[1]
    gate_off = 7 * GROUP_W
    s_lo, s_hi = N_META, N_META + n_dec

    x_main = x_prompt[0]
    x_tail = jnp.concatenate([meta_tokens.astype(F32), x_sample[:, 0],
                              jnp.zeros((TAIL_ROWS - s_hi, d), F32)], axis=0)
    w_in0 = jnp.swapaxes(w_in[0], 0, 1)
    wg = jnp.zeros((2 * HEAD_W, d), F32)
    wg = wg.at[:N_HEADS].set(w_in0[gate_off:gate_off + N_HEADS])
    wg = wg.at[HEAD_W:HEAD_W + N_HEADS].set(w_in0[gate_off + N_HEADS:gate_off + 2 * N_HEADS])
    wg = wg.astype(BF16)
    gbias = jnp.zeros((1, 2 * HEAD_W), F32)
    gbias = gbias.at[0, :N_HEADS].set(b_i[0]).at[0, HEAD_W:HEAD_W + N_HEADS].set(b_f[0])
    lam4 = jnp.stack([lam_q1[0], lam_k1[0], lam_q2[0], lam_k2[0]]).astype(F32)
    g_da2 = g_da[0].reshape(1, HEAD_W)
    g_ml2 = g_ml[0].reshape(1, GROUP_W)
    g_mix2 = g_mix[0].reshape(1, d)
    g_ffn2 = g_ffn[0].reshape(1, d)
    g_fin2 = g_final.reshape(1, d)
    conv_w2 = conv_w[0]
    conv_b2 = conv_b[0].reshape(1, 2 * d_ff)
    k_scale = HEAD_W ** -0.5
    q_scale = DA_QK ** -0.5 * LOG2E

    other_groups = ((0, 3, 4, 5, 6), (q_scale, 1.0, k_scale, 1.0, 1.0))

    h_t, ig_t, lf_t = _pre(x_tail, g_mix2, wg, gbias)
    kv_f_t, kv_b_t = _proj(h_t, w_in0, (1, 2), (1.0, 1.0), True)
    oth_t = _proj(h_t, w_in0, *other_groups, False)
    pt = dict(kv_b=kv_b_t, oth=oth_t, ig=ig_t, lf=lf_t, k_f=kv_f_t[0], v_f=kv_f_t[1],
              dq=oth_t[0], mq=oth_t[1], mk=oth_t[2], mv=oth_t[3], mo=oth_t[4])

    h_m, ig_m, lf_m = _pre(x_main, g_mix2, wg, gbias)
    k_rows, k_b_m = _proj_rows(h_m, w_in0, 1, pt["k_f"][:N_META])
    v_rows, v_b_m = _proj_rows(h_m, w_in0, 2, pt["v_f"][:N_META])
    pm = dict(oth=_proj(h_m, w_in0, *other_groups, False), ig=ig_m, lf=lf_m)

    tk = _pick(seq, 512)
    tq = _pick(seq, 1024)
    pairs = [(i, j) for i in range(seq // tq) for j in range((i + 1) * tq // tk)]
    kv_main = ((k_b_m[None], 0), (v_b_m[None], 0))
    kv_meta = ((pt["kv_b"], 0), (pt["kv_b"], 1))
    da_main = _attn((pm["oth"], 0), *kv_main, *kv_meta, lam4, g_da2,
                    n_q_rows=seq, tq=tq, tk=tk, q_pos0=0, prefix_off=FAR, pairs=pairs)
    da_meta = _attn((pt["oth"], 0), *kv_main, *kv_meta, lam4, g_da2,
                    n_q_rows=N_META, tq=N_META, tk=tk, q_pos0=-FAR, prefix_off=0, pairs=[(0, 0)])

    c_zero = jnp.zeros((N_HEADS, HEAD_W, 2 * HEAD_W), F32)
    m_zero = jnp.zeros((1, HEAD_W), F32)
    ml_meta, c_meta, m_meta = _mlstm(pt["oth"], pt["ig"], pt["lf"], c_zero, m_zero, g_ml2,
                                     n_rows=N_META, chunk=N_META)
    ml_main, c_fin, m_fin = _mlstm(pm["oth"], pm["ig"], pm["lf"], c_meta, m_meta, g_ml2,
                                   n_rows=seq, chunk=_pick(seq, 256))

    f32_rows = lambda a: a[s_lo:s_hi].astype(F32)
    heads = lambda a: a.reshape(n_dec, N_HEADS, HEAD_W)
    twice = lambda a: jnp.concatenate([heads(a), heads(a)], axis=1)
    ck = cache_k[0].reshape(n_pool, PAGE_SIZE * N_HEADS, HEAD_W)
    cv = cache_v[0].reshape(n_pool, PAGE_SIZE * N_HEADS, HEAD_W)
    da_smp = _decode_attn(page_table, twice(f32_rows(pt["dq"])), twice(pt["k_f"][s_lo:s_hi]),
                          twice(pt["v_f"][s_lo:s_hi]), ck, cv, lam4, g_da2)
    da_smp = da_smp.reshape(n_dec, GROUP_W).astype(BF16)
    m_pad = jnp.zeros((n_dec, 1, HEAD_W), F32).at[:, 0, :N_HEADS].set(state_m[0].astype(F32))
    ml_smp, c_smp, n_smp, m_smp = _mlstm_step(
        heads(f32_rows(pt["mq"])), heads(f32_rows(pt["mk"])), heads(f32_rows(pt["mv"])),
        heads(pt["mo"][s_lo:s_hi]), pt["ig"][s_lo:s_hi, None, :], pt["lf"][s_lo:s_hi, None, :],
        state_C[0].astype(F32), state_n[0].astype(F32), m_pad, g_ml2.reshape(N_HEADS, HEAD_W))
    ml_smp = ml_smp.reshape(n_dec, GROUP_W).astype(BF16)

    pad_rows = jnp.zeros((TAIL_ROWS - s_hi, GROUP_W), BF16)
    da_tail = jnp.concatenate([da_meta, da_smp, pad_rows], axis=0)
    ml_tail = jnp.concatenate([ml_meta, ml_smp, pad_rows], axis=0)
    x1_main = _out_proj(da_main, ml_main, x_main, w_out[0])
    x1_tail = _out_proj(da_tail, ml_tail, x_tail, w_out[0])

    hist = jnp.zeros((2, TAIL_ROWS, 2 * d_ff), F32).at[:, s_lo:s_hi].set(
        jnp.swapaxes(state_conv[0].astype(F32), 0, 1))
    y_tail, u_tail, (wa_b, wg_b, wd_b) = _ffn(x1_tail, g_ffn2, g_fin2, w_up[0], w_up[0], w_down[0],
                                              conv_w2, conv_b2, s0=hist[0], s1=hist[1])
    y_main, u_last, _ = _ffn(x1_main, g_ffn2, g_fin2, wa_b, wg_b, wd_b, conv_w2, conv_b2,
                             prefix=u_tail[N_META - 8:N_META])

    y_prompt = y_main[None]
    y_sample = y_tail[s_lo:s_hi, None, :]
    k_rows_p = k_rows[None, None]
    v_rows_p = v_rows[None, None]
    k_rows_s = pt["k_f"][s_lo:s_hi].reshape(1, n_dec, 1, N_HEADS, HEAD_W)
    v_rows_s = pt["v_f"][s_lo:s_hi].reshape(1, n_dec, 1, N_HEADS, HEAD_W)
    c_p = c_fin[None, None, :, :, :HEAD_W]
    n_p = c_fin[None, None, :, :, HEAD_W]
    m_p = m_fin[None, :, :N_HEADS]
    c_s = c_smp[None]
    n_s = n_smp[None]
    m_s = m_smp[None, :, 0, :N_HEADS]
    conv_p = u_last[None, None, -2:, :]
    conv_s = jnp.stack([state_conv[0][:, 1].astype(F32), u_tail[s_lo:s_hi]], axis=1)[None]
    return (y_prompt, y_sample, k_rows_p, v_rows_p, k_rows_s, v_rows_s,
            c_p, n_p, m_p, c_s, n_s, m_s, conv_p, conv_s)
```
